```python
import math
import numpy as np
import jax
import jax.numpy as jnp
from jax import lax

D_MODEL = 1024
BATCH = 1
SEQ = 16384
DEPTH = 1
DEC_BATCH = 128
DEC_SEQ = 1
PAST_LEN = 16384
PAGE_SIZE = 128

D_FF = 2816
NSA_HEADS = 8
NSA_GROUPS = 2
NSA_HPG = NSA_HEADS // NSA_GROUPS
NSA_DK = 64
CMP_BLOCK = 32
CMP_HIDDEN = 128
SEL_BLOCK = 64
SEL_RATIO = SEL_BLOCK // CMP_BLOCK
N_SEL = 16
WINDOW = 512
MLA_HEADS = 8
Q_LORA = 256
KV_LORA = 128
D_NOPE = 64
D_ROPE = 32
D_V = 64
ROPE_THETA = 10000.0
MLA_ROW = KV_LORA + D_ROPE
MEM_TOKENS = 256
MEM_HEADS = 4
MEM_DH = 128
N_BRANCH = 3
BRANCH_W = 512
N_BUCKETS = 32
MAX_DISTANCE = 128
Q_BLOCK = 128
ALPHA = (2.0 * DEPTH) ** 0.25
BETA = (8.0 * DEPTH) ** -0.25
LN_EPS = 1e-5
RMS_EPS = 1e-6
IN_WIDTHS = (NSA_HEADS * NSA_DK, 6 * NSA_GROUPS * NSA_DK, 3 * NSA_HEADS, Q_LORA, KV_LORA, D_ROPE,
             MEM_HEADS * MEM_DH, N_BRANCH * D_MODEL)
N_IN = sum(IN_WIDTHS)
NSA_AXES = (0, 0, None, 0, 0, 0, 0, 0, 0, None, None)

kernel_name = "hybrid_nsa_mla_mem_macaron_step"


def layer_norm(x, g, b):
    xf = x.astype(jnp.float32)
    mu = jnp.mean(xf, -1, keepdims=True)
    var = jnp.mean(jnp.square(xf - mu), -1, keepdims=True)
    return ((xf - mu) * lax.rsqrt(var + LN_EPS) * g + b).astype(x.dtype)


def rms_norm(x, g):
    xf = x.astype(jnp.float32)
    return (xf * lax.rsqrt(jnp.mean(jnp.square(xf), -1, keepdims=True) + RMS_EPS) * g).astype(x.dtype)


def masked_softmax(s, mask):
    s = jnp.where(mask, s.astype(jnp.float32), -jnp.inf)
    m = jnp.max(s, axis=-1, keepdims=True)
    m = jnp.where(jnp.isfinite(m), m, 0.0)
    e = jnp.exp(s - m)
    d = jnp.sum(e, axis=-1, keepdims=True)
    return e / jnp.where(d > 0.0, d, 1.0)


def t5_bucket(dist):
    n = jnp.maximum(dist, 0)
    max_exact = N_BUCKETS // 2
    nf = jnp.maximum(n, 1).astype(jnp.float32)
    large = max_exact + (jnp.log(nf / max_exact) / math.log(MAX_DISTANCE / max_exact)
                         * (N_BUCKETS - max_exact)).astype(jnp.int32)
    large = jnp.minimum(large, N_BUCKETS - 1)
    return jnp.where(n < max_exact, n, large)


def head_bias(table, dist):
    tq, n = dist.shape
    b = table[t5_bucket(dist)].astype(jnp.float32)
    return b.reshape(tq, n, NSA_GROUPS, NSA_HPG).transpose(0, 2, 3, 1)


def rope(x, pos):
    half = D_ROPE // 2
    freq = ROPE_THETA ** (-jnp.arange(half, dtype=jnp.float32) / half)
    ang = pos.astype(jnp.float32)[:, None] * freq
    shape = (1, ang.shape[0]) + (1,) * (x.ndim - 3) + (half,)
    cos = jnp.cos(ang).reshape(shape)
    sin = jnp.sin(ang).reshape(shape)
    xf = x.astype(jnp.float32)
    x1, x2 = xf[..., :half], xf[..., half:]
    return jnp.concatenate([x1 * cos - x2 * sin, x1 * sin + x2 * cos], -1).astype(x.dtype)


def pad_rows(a, mult):
    extra = (-a.shape[1]) % mult
    return jnp.pad(a, [(0, 0), (0, extra)] + [(0, 0)] * (a.ndim - 2))


def ffn_sublayer(x, w1, w3, w2, g, b):
    h = jax.nn.silu(x @ w1) * (x @ w3)
    return layer_norm(ALPHA * x + 0.5 * (h @ w2), g, b)


def nsa_compress(k, pos_emb, w1, w2):
    b, l, g, d = k.shape
    nc = l // CMP_BLOCK
    kb = k[:, :nc * CMP_BLOCK].reshape(b, nc, CMP_BLOCK, g, d) + pos_emb[None, None, :, None, :]
    flat = kb.transpose(0, 1, 3, 2, 4).reshape(b, nc, g, CMP_BLOCK * d)
    return jax.nn.silu(flat @ w1) @ w2


def nsa_attend(q, gate, q_pos, kc, vc, ks, vs, kw, vw, kw_pos, rel_table):
    tq = q.shape[0]
    scale = NSA_DK ** -0.5
    qg = q.reshape(tq, NSA_GROUPS, NSA_HPG, NSA_DK)
    nc = kc.shape[0]
    cmp_end = jnp.arange(nc) * CMP_BLOCK + (CMP_BLOCK - 1)
    dist_c = q_pos[:, None] - cmp_end[None, :]
    s_c = jnp.einsum('tghd,ngd->tghn', qg, kc).astype(jnp.float32) * scale + head_bias(rel_table, dist_c)
    p_c = masked_softmax(s_c, (dist_c >= 0)[:, None, None, :])
    o_c = jnp.einsum('tghn,ngd->tghd', p_c.astype(vc.dtype), vc)
    n_blk = ks.shape[0] // SEL_BLOCK
    imp = jnp.sum(p_c, axis=2)
    imp = jnp.pad(imp, ((0, 0), (0, 0), (0, n_blk * SEL_RATIO - nc)))
    imp = imp.reshape(tq, NSA_GROUPS, n_blk, SEL_RATIO).sum(-1)
    blk = jnp.arange(n_blk)[None, :]
    cur = (q_pos // SEL_BLOCK)[:, None]
    valid = blk * SEL_BLOCK <= q_pos[:, None]
    forced = valid & ((blk == 0) | (blk == cur) | (blk == cur - 1))
    score = jnp.where(forced[:, None, :], NSA_HPG + 1.0, jnp.where(valid[:, None, :], imp, -1.0))
    n_sel = min(N_SEL, n_blk)
    _, idx = lax.top_k(score, n_sel)
    tok = (idx[..., None] * SEL_BLOCK + jnp.arange(SEL_BLOCK)).reshape(tq, NSA_GROUPS, n_sel * SEL_BLOCK)
    g_idx = jnp.arange(NSA_GROUPS)[None, :, None]
    ksg = jnp.transpose(ks, (1, 0, 2))[g_idx, tok]
    vsg = jnp.transpose(vs, (1, 0, 2))[g_idx, tok]
    dist_s = q_pos[:, None, None] - tok
    tab = rel_table.reshape(N_BUCKETS, NSA_GROUPS, NSA_HPG)
    b_s = jnp.transpose(tab[t5_bucket(dist_s), g_idx], (0, 1, 3, 2)).astype(jnp.float32)
    s_s = jnp.einsum('tghd,tgkd->tghk', qg, ksg).astype(jnp.float32) * scale + b_s
    p_s = masked_softmax(s_s, (dist_s >= 0)[:, :, None, :])
    o_s = jnp.einsum('tghk,tgkd->tghd', p_s.astype(vsg.dtype), vsg)
    dist_w = q_pos[:, None] - kw_pos[None, :]
    s_w = jnp.einsum('tghd,lgd->tghl', qg, kw).astype(jnp.float32) * scale + head_bias(rel_table, dist_w)
    mask_w = (dist_w >= 0) & (dist_w <= WINDOW) & (kw_pos[None, :] >= 0)
    p_w = masked_softmax(s_w, mask_w[:, None, None, :])
    o_w = jnp.einsum('tghl,lgd->tghd', p_w.astype(vw.dtype), vw)
    g = gate.reshape(tq, NSA_GROUPS, NSA_HPG, 3)
    o = g[..., 0:1] * o_c + g[..., 1:2] * o_s + g[..., 2:3] * o_w
    return o.reshape(tq, NSA_HEADS * NSA_DK)


def nsa_prompt(q, gate, kv, cmp_pos, cmp_w1, cmp_w2, rel_table):
    b, s = q.shape[:2]
    kc = nsa_compress(kv[:, :, 0], cmp_pos[0], cmp_w1[0], cmp_w2[0])
    vc = nsa_compress(kv[:, :, 1], cmp_pos[1], cmp_w1[1], cmp_w2[1])
    ks = pad_rows(kv[:, :, 2], SEL_BLOCK)
    vs = pad_rows(kv[:, :, 3], SEL_BLOCK)
    kw_pad = jnp.pad(kv[:, :, 4:6], ((0, 0), (WINDOW, 0), (0, 0), (0, 0), (0, 0)))
    band = WINDOW + Q_BLOCK

    def one_block(i):
        start = i * Q_BLOCK
        qb = lax.dynamic_slice_in_dim(q, start, Q_BLOCK, axis=1)
        gb = lax.dynamic_slice_in_dim(gate, start, Q_BLOCK, axis=1)
        wb = lax.dynamic_slice_in_dim(kw_pad, start, band, axis=1)
        q_pos = start + jnp.arange(Q_BLOCK)
        kw_pos = start - WINDOW + jnp.arange(band)
        return jax.vmap(nsa_attend, in_axes=NSA_AXES)(qb, gb, q_pos, kc, vc, ks, vs,
                                                       wb[:, :, 0], wb[:, :, 1], kw_pos, rel_table)

    out = lax.map(one_block, jnp.arange(s // Q_BLOCK))
    return out.transpose(1, 0, 2, 3).reshape(b, s, NSA_HEADS * NSA_DK)


def nsa_sample(q, gate, kv_new, past_kv, win_state, past, cmp_pos, cmp_w1, cmp_w2, rel_table):
    t = q.shape[1]
    full = jnp.concatenate([past_kv, kv_new[:, :, :4]], axis=1)
    kc = nsa_compress(full[:, :, 0], cmp_pos[0], cmp_w1[0], cmp_w2[0])
    vc = nsa_compress(full[:, :, 1], cmp_pos[1], cmp_w1[1], cmp_w2[1])
    ks = pad_rows(full[:, :, 2], SEL_BLOCK)
    vs = pad_rows(full[:, :, 3], SEL_BLOCK)
    win = jnp.concatenate([win_state, kv_new[:, :, 4:6]], axis=1)
    wb = win_state.shape[1]
    kw_pos = past - wb + jnp.arange(wb + t)
    q_pos = past + jnp.arange(t)
    o = jax.vmap(nsa_attend, in_axes=NSA_AXES)(q, gate, q_pos, kc, vc, ks, vs,
                                                win[:, :, 0], win[:, :, 1], kw_pos, rel_table)
    return o, win[:, t:]


def mla_attend(q_lat, q_rope, ckv, krope, q_pos, k_pos):
    s = (jnp.einsum('bthr,blr->bhtl', q_lat, ckv) + jnp.einsum('bthd,bld->bhtl', q_rope, krope))
    s = s.astype(jnp.float32) * (D_NOPE + D_ROPE) ** -0.5
    p = masked_softmax(s, k_pos[None, :] <= q_pos[:, None])
    return jnp.einsum('bhtl,blr->bthr', p.astype(ckv.dtype), ckv)


def mla_prompt(q_lat, q_rope, rows):
    b, s = rows.shape[:2]
    ckv, krope = rows[..., :KV_LORA], rows[..., KV_LORA:]
    k_pos = jnp.arange(s)

    def one_block(i):
        start = i * Q_BLOCK
        ql = lax.dynamic_slice_in_dim(q_lat, start, Q_BLOCK, axis=1)
        qr = lax.dynamic_slice_in_dim(q_rope, start, Q_BLOCK, axis=1)
        return mla_attend(ql, qr, ckv, krope, start + jnp.arange(Q_BLOCK), k_pos)

    out = lax.map(one_block, jnp.arange(s // Q_BLOCK))
    return out.transpose(1, 0, 2, 3, 4).reshape(b, s, MLA_HEADS, KV_LORA)


def mem_attend(q, k, v):
    s = jnp.einsum('bthd,bmhd->bhtm', q, k).astype(jnp.float32) * MEM_DH ** -0.5
    p = jax.nn.softmax(s, axis=-1)
    return jnp.einsum('bhtm,bmhd->bthd', p.astype(v.dtype), v)


def mixer_project(x, pos, w_in, g_q, w_uq, w_qr, g_kv, w_uk):
    b, t, _ = x.shape
    h = x @ w_in
    nq, nkv, ng, qd, kvd, kr, mq, mg = jnp.split(h, np.cumsum(IN_WIDTHS)[:-1].tolist(), axis=-1)
    nq = nq.reshape(b, t, NSA_HEADS, NSA_DK)
    nkv = nkv.reshape(b, t, 6, NSA_GROUPS, NSA_DK)
    ng = jax.nn.sigmoid(ng.reshape(b, t, NSA_HEADS, 3))
    c_q = rms_norm(qd, g_q)
    q_nope = jnp.einsum('btc,chd->bthd', c_q, w_uq)
    q_rope = rope(jnp.einsum('btc,chd->bthd', c_q, w_qr), pos)
    q_lat = jnp.einsum('bthd,rhd->bthr', q_nope, w_uk)
    c_kv = rms_norm(kvd, g_kv)
    k_rope = rope(kr, pos)
    mla_row = jnp.concatenate([c_kv, k_rope], axis=-1)
    mq = mq.reshape(b, t, MEM_HEADS, MEM_DH)
    return nq, nkv, ng, q_lat, q_rope, mla_row, mq, mg


def mixer_merge(o_nsa, o_mla_lat, o_mem, merge_g, w_uv, w_br, w_o):
    b, t = o_nsa.shape[:2]
    v_mla = jnp.einsum('bthr,rhd->bthd', o_mla_lat, w_uv).reshape(b, t, BRANCH_W)
    branches = jnp.stack([o_nsa, v_mla, o_mem.reshape(b, t, BRANCH_W)], axis=2)
    proj = jnp.einsum('btnw,nwd->btnd', branches, w_br)
    gates = jax.nn.sigmoid(merge_g.reshape(b, t, N_BRANCH, D_MODEL))
    return jnp.sum(gates * proj, axis=2) @ w_o


def setup_inputs(seed: int = 0) -> dict:
    key = jax.random.key(seed)
    keys = iter(jax.random.split(key, 32))

    def nrm(shape, scale=1.0):
        return jax.random.normal(next(keys), shape, jnp.float32) * scale

    n_pages = PAST_LEN // PAGE_SIZE
    n_used = DEC_BATCH * n_pages
    n_pool = n_used + max(1, n_used // 4)
    win_buf = min(WINDOW, PAST_LEN)
    x_prompt = nrm((BATCH, SEQ, D_MODEL))
    x_sample = nrm((DEC_BATCH, DEC_SEQ, D_MODEL))
    mem_prompt = nrm((BATCH, MEM_TOKENS, D_MODEL))
    cache_nsa_kv = nrm((DEPTH, n_pool, PAGE_SIZE, 4, NSA_GROUPS, NSA_DK))
    cache_mla = nrm((DEPTH, n_pool, PAGE_SIZE, MLA_ROW))
    state_nsa_win = nrm((DEPTH, DEC_BATCH, win_buf, 2, NSA_GROUPS, NSA_DK))
    cache_mem_kv = nrm((DEPTH, DEC_BATCH, MEM_TOKENS, 2, MEM_HEADS, MEM_DH))
    perm = jax.random.permutation(next(keys), n_pool)
    page_table = perm[:n_used].reshape(DEC_BATCH, n_pages).astype(jnp.int32)
    return {
        'x_prompt': x_prompt,
        'x_sample': x_sample,
        'mem_prompt': mem_prompt,
        'cache_nsa_kv': cache_nsa_kv,
        'cache_mla': cache_mla,
        'state_nsa_win': state_nsa_win,
        'cache_mem_kv': cache_mem_kv,
        'page_table': page_table,
        'rel_bias': nrm((N_BUCKETS, NSA_HEADS), 0.3),
        'ln_g': 1.0 + nrm((DEPTH, 3, D_MODEL), 0.02),
        'ln_b': nrm((DEPTH, 3, D_MODEL), 0.02),
        'ffn_w1': nrm((DEPTH, 2, D_MODEL, D_FF), D_MODEL ** -0.5),
        'ffn_w3': nrm((DEPTH, 2, D_MODEL, D_FF), D_MODEL ** -0.5),
        'ffn_w2': nrm((DEPTH, 2, D_FF, D_MODEL), BETA * D_FF ** -0.5),
        'w_in': nrm((DEPTH, D_MODEL, N_IN), D_MODEL ** -0.5),
        'nsa_cmp_pos': nrm((DEPTH, 2, CMP_BLOCK, NSA_DK), 0.1),
        'nsa_cmp_w1': nrm((DEPTH, 2, CMP_BLOCK * NSA_DK, CMP_HIDDEN), (CMP_BLOCK * NSA_DK) ** -0.5),
        'nsa_cmp_w2': nrm((DEPTH, 2, CMP_HIDDEN, NSA_DK), CMP_HIDDEN ** -0.5),
        'mla_g_q': 1.0 + nrm((DEPTH, Q_LORA), 0.02),
        'mla_w_uq': nrm((DEPTH, Q_LORA, MLA_HEADS, D_NOPE), Q_LORA ** -0.5),
        'mla_w_qr': nrm((DEPTH, Q_LORA, MLA_HEADS, D_ROPE), Q_LORA ** -0.5),
        'mla_g_kv': 1.0 + nrm((DEPTH, KV_LORA), 0.02),
        'mla_w_uk': nrm((DEPTH, KV_LORA, MLA_HEADS, D_NOPE), KV_LORA ** -0.5),
        'mla_w_uv': nrm((DEPTH, KV_LORA, MLA_HEADS, D_V), KV_LORA ** -0.5),
        'mem_w_kv': nrm((DEPTH, D_MODEL, 2, MEM_HEADS, MEM_DH), D_MODEL ** -0.5),
        'w_br': nrm((DEPTH, N_BRANCH, BRANCH_W, D_MODEL), BRANCH_W ** -0.5),
        'w_o': nrm((DEPTH, D_MODEL, D_MODEL), BETA * D_MODEL ** -0.5),
    }


def reference(x_prompt, x_sample, mem_prompt, cache_nsa_kv, cache_mla, state_nsa_win, cache_mem_kv,
              page_table, rel_bias, ln_g, ln_b, ffn_w1, ffn_w3, ffn_w2, w_in, nsa_cmp_pos, nsa_cmp_w1,
              nsa_cmp_w2, mla_g_q, mla_w_uq, mla_w_qr, mla_g_kv, mla_w_uk, mla_w_uv, mem_w_kv, w_br, w_o):
    s = x_prompt.shape[1]
    db, t = x_sample.shape[:2]
    n_pages = page_table.shape[1]
    past = n_pages * cache_nsa_kv.shape[2]
    pos_p = jnp.arange(s)
    pos_s = past + jnp.arange(t)
    k_pos_s = jnp.arange(past + t)
    xp, xs = x_prompt, x_sample
    p_nsa, p_mla, p_win, p_mem, s_nsa, s_mla, s_win = [], [], [], [], [], [], []
    for l in range(DEPTH):
        xp = ffn_sublayer(xp, ffn_w1[l, 0], ffn_w3[l, 0], ffn_w2[l, 0], ln_g[l, 0], ln_b[l, 0])
        xs = ffn_sublayer(xs, ffn_w1[l, 0], ffn_w3[l, 0], ffn_w2[l, 0], ln_g[l, 0], ln_b[l, 0])
        nq, nkv, ng, q_lat, q_rope, mla_row, mq, mg = mixer_project(
            xp, pos_p, w_in[l], mla_g_q[l], mla_w_uq[l], mla_w_qr[l], mla_g_kv[l], mla_w_uk[l])
        o_nsa = nsa_prompt(nq, ng, nkv, nsa_cmp_pos[l], nsa_cmp_w1[l], nsa_cmp_w2[l], rel_bias)
        o_mla = mla_prompt(q_lat, q_rope, mla_row)
        mem_kv = jnp.einsum('bmd,dkhe->bmkhe', mem_prompt, mem_w_kv[l])
        o_mem = mem_attend(mq, mem_kv[:, :, 0], mem_kv[:, :, 1])
        mix = mixer_merge(o_nsa, o_mla, o_mem, mg, mla_w_uv[l], w_br[l], w_o[l])
        xp = layer_norm(ALPHA * xp + mix, ln_g[l, 1], ln_b[l, 1])
        p_nsa.append(nkv[:, :, :4])
        p_mla.append(mla_row)
        p_win.append(nkv[:, s - min(WINDOW, s):, 4:6])
        p_mem.append(mem_kv)
        nq, nkv, ng, q_lat, q_rope, mla_row, mq, mg = mixer_project(
            xs, pos_s, w_in[l], mla_g_q[l], mla_w_uq[l], mla_w_qr[l], mla_g_kv[l], mla_w_uk[l])
        past_nsa = cache_nsa_kv[l, page_table].reshape(db, past, 4, NSA_GROUPS, NSA_DK)
        past_mla = cache_mla[l, page_table].reshape(db, past, MLA_ROW)
        o_nsa, new_win = nsa_sample(nq, ng, nkv, past_nsa, state_nsa_win[l], past,
                                    nsa_cmp_pos[l], nsa_cmp_w1[l], nsa_cmp_w2[l], rel_bias)
        full_mla = jnp.concatenate([past_mla, mla_row], axis=1)
        o_mla = mla_attend(q_lat, q_rope, full_mla[..., :KV_LORA], full_mla[..., KV_LORA:], pos_s, k_pos_s)
        o_mem = mem_attend(mq, cache_mem_kv[l, :, :, 0], cache_mem_kv[l, :, :, 1])
        mix = mixer_merge(o_nsa, o_mla, o_mem, mg, mla_w_uv[l], w_br[l], w_o[l])
        xs = layer_norm(ALPHA * xs + mix, ln_g[l, 1], ln_b[l, 1])
        s_nsa.append(nkv[:, :, :4])
        s_mla.append(mla_row)
        s_win.append(new_win)
        xp = ffn_sublayer(xp, ffn_w1[l, 1], ffn_w3[l, 1], ffn_w2[l, 1], ln_g[l, 2], ln_b[l, 2])
        xs = ffn_sublayer(xs, ffn_w1[l, 1], ffn_w3[l, 1], ffn_w2[l, 1], ln_g[l, 2], ln_b[l, 2])
    new_p_nsa = jnp.stack(p_nsa)
    new_p_mla = jnp.stack(p_mla)
    new_p_win = jnp.stack(p_win)
    new_p_mem = jnp.stack(p_mem)
    new_s_nsa = jnp.stack(s_nsa)
    new_s_mla = jnp.stack(s_mla)
    new_s_win = jnp.stack(s_win)
    return (xp, xs, new_p_nsa, new_p_mla, new_p_win, new_p_mem, new_s_nsa, new_s_mla, new_s_win)
```

```python
import functools
import math

import jax
import jax.numpy as jnp
import numpy as np
from jax import lax
from jax.experimental import pallas as pl
from jax.experimental.pallas import tpu as pltpu

F32 = jnp.float32
BF16 = jnp.bfloat16

D_MODEL = 1024
D_FF = 2816
NSA_HEADS = 8
NSA_GROUPS = 2
NSA_HPG = 4
NSA_DK = 64
CMP_BLOCK = 32
CMP_HIDDEN = 128
SEL_BLOCK = 64
N_SEL = 16
WINDOW = 512
MLA_HEADS = 8
Q_LORA = 256
KV_LORA = 128
D_NOPE = 64
D_ROPE = 32
D_V = 64
ROPE_THETA = 10000.0
MLA_ROW = KV_LORA + D_ROPE
MEM_TOKENS = 256
MEM_HEADS = 4
MEM_DH = 128
N_BRANCH = 3
BRANCH_W = 512
N_BUCKETS = 32
MAX_DISTANCE = 128
PAGE = 128
ALPHA = 2.0 ** 0.25
LN_EPS = 1e-5
RMS_EPS = 1e-6
IN_WIDTHS = (512, 768, 24, 256, 128, 32, 512, 3072)

LANES = 128
FF_CHUNK = 256
NEG = -1e30
M_INIT = -1e29
VMEM_LIMIT = 56 * 1024 * 1024


def _cparams(n_axes):
    return pltpu.CompilerParams(dimension_semantics=("arbitrary",) * n_axes,
                                vmem_limit_bytes=VMEM_LIMIT)


def _full(shape):
    n = len(shape)
    return pl.BlockSpec(shape, lambda *_: (0,) * n)


def _dot(a, b):
    return jnp.dot(a, b, preferred_element_type=F32)


def _dot_nt(a, b):
    return lax.dot_general(a, b, (((1,), (1,)), ((), ())), preferred_element_type=F32)


def _layer_norm(y, g, b):
    mu = jnp.mean(y, axis=-1, keepdims=True)
    yc = y - mu
    var = jnp.mean(yc * yc, axis=-1, keepdims=True)
    return yc * lax.rsqrt(var + LN_EPS) * g + b


def _ffn_body(x_ref, w1_ref, w3_ref, w2_ref, g_ref, b_ref, o_ref):
    x = x_ref[...]
    xb = x.astype(BF16)
    acc = jnp.zeros(x.shape, F32)
    for c in range(D_FF // FF_CHUNK):
        sl = slice(c * FF_CHUNK, (c + 1) * FF_CHUNK)
        a = _dot(xb, w1_ref[:, sl])
        b = _dot(xb, w3_ref[:, sl])
        h = (a * jax.nn.sigmoid(a) * b).astype(BF16)
        acc = acc + _dot(h, w2_ref[sl, :])
    o_ref[...] = _layer_norm(ALPHA * x + 0.5 * acc, g_ref[...], b_ref[...])


def _ffn(x, w1, w3, w2, g, b, tm):
    rows = x.shape[0]
    return pl.pallas_call(
        _ffn_body,
        grid=(rows // tm,),
        in_specs=[pl.BlockSpec((tm, D_MODEL), lambda i: (i, 0)),
                  _full(w1.shape), _full(w3.shape), _full(w2.shape), _full(g.shape), _full(b.shape)],
        out_specs=pl.BlockSpec((tm, D_MODEL), lambda i: (i, 0)),
        out_shape=jax.ShapeDtypeStruct((rows, D_MODEL), F32),
        compiler_params=_cparams(1),
        name="ffn",
    )(x, w1, w3, w2, g, b)


P_NQ, P_NKV, P_NG, P_QD, P_KVD, P_KR, P_MQ = 0, 512, 1280, 1408, 1664, 1792, 1920
P_WIDTH = 2432


def _rope(x, cos, s_lo, s_hi):
    return (x * cos + pltpu.roll(x, LANES - D_ROPE // 2, 1) * s_lo
            + pltpu.roll(x, D_ROPE // 2, 1) * s_hi)


def _rms(x, g):
    return x * lax.rsqrt(jnp.mean(x * x, axis=-1, keepdims=True) + RMS_EPS) * g


def _proj_body(prompt, x_ref, wp_ref, wuq_ref, wuk_ref, wqr_ref, gq_ref, gkv_ref,
               cos_ref, slo_ref, shi_ref, *outs):
    if prompt:
        (qn_ref, nkv4_ref, win_ref, gate_ref, qmla_ref, row_ref, kmla_ref, mq_ref,
         kslc_ref, vslct_ref, kwin_ref, vwint_ref) = outs
    else:
        qn_ref, nkv4_ref, win_ref, gate_ref, qmla_ref, row_ref, kmla_ref, mq_ref = outs
    xb = x_ref[...].astype(BF16)
    cos, s_lo, s_hi = cos_ref[...], slo_ref[...], shi_ref[...]

    def seg(start, width):
        return _dot(xb, wp_ref[:, start:start + width])

    hq = (seg(P_NQ, 512) * (NSA_DK ** -0.5)).astype(BF16)
    if prompt:
        for h in range(NSA_HEADS):
            qn_ref[h] = hq[:, h * NSA_DK:(h + 1) * NSA_DK]
    else:
        qn_ref[...] = hq
    nkv = seg(P_NKV, 768)
    nkv4_ref[...] = nkv[:, :512]
    win_ref[...] = nkv[:, 512:768]
    if prompt:
        tm = nkv.shape[0]
        nkvb = nkv.astype(BF16)
        for g in range(NSA_GROUPS):
            kslc_ref[g] = nkvb[:, 256 + g * NSA_DK:256 + (g + 1) * NSA_DK]
            kwin_ref[g] = nkvb[:, 512 + g * NSA_DK:512 + (g + 1) * NSA_DK]
        vt = nkv[:, 384:512].T.astype(BF16)
        for c in range(tm // 256):
            vslct_ref[c] = vt[:, c * 256:(c + 1) * 256]
        wt = nkv[:, 640:768].T.astype(BF16)
        for c in range(tm // 128):
            vwint_ref[c] = wt[:, c * 128:(c + 1) * 128]
    gate_ref[...] = jax.nn.sigmoid(seg(P_NG, 128))
    cq = _rms(seg(P_QD, 256), gq_ref[...]).astype(BF16)
    q_nope = _dot(cq, wuq_ref[...]).astype(BF16)
    q_lat = _dot(q_nope, wuk_ref[...])
    q_rope = _dot(cq, wqr_ref[...])
    for h in range(MLA_HEADS):
        sl = slice(h * LANES, (h + 1) * LANES)
        qmla_ref[:, 2 * h * LANES:(2 * h + 1) * LANES] = q_lat[:, sl].astype(BF16)
        qmla_ref[:, (2 * h + 1) * LANES:(2 * h + 2) * LANES] = _rope(q_rope[:, sl], cos, s_lo, s_hi).astype(BF16)
    ckv = _rms(seg(P_KVD, 128), gkv_ref[...])
    kr = _rope(seg(P_KR, 128), cos, s_lo, s_hi)
    row_ref[:, :KV_LORA] = ckv
    row_ref[:, KV_LORA:] = kr[:, :D_ROPE]
    kmla_ref[:, :KV_LORA] = ckv.astype(BF16)
    kmla_ref[:, KV_LORA:] = kr.astype(BF16)
    mq_ref[...] = seg(P_MQ, 512).astype(BF16)


def _proj(x, wts, tables, tm, prompt):
    rows = x.shape[0]
    wp, wuq, wuk, wqr, gq, gkv = wts
    cos, s_lo, s_hi = tables
    row_spec = lambda w: pl.BlockSpec((tm, w), lambda i: (i, 0))
    out_shapes = [
        jax.ShapeDtypeStruct((NSA_HEADS, rows, NSA_DK) if prompt else (rows, 512), BF16),
        jax.ShapeDtypeStruct((rows, 512), F32),
        jax.ShapeDtypeStruct((rows, 256), F32),
        jax.ShapeDtypeStruct((rows, LANES), F32),
        jax.ShapeDtypeStruct((rows, 2048), BF16),
        jax.ShapeDtypeStruct((rows, MLA_ROW), F32),
        jax.ShapeDtypeStruct((rows, 256), BF16),
        jax.ShapeDtypeStruct((rows, 512), BF16),
    ]
    out_specs = [
        pl.BlockSpec((NSA_HEADS, tm, NSA_DK), lambda i: (0, i, 0)) if prompt else row_spec(512),
        row_spec(512), row_spec(256), row_spec(LANES), row_spec(2048), row_spec(MLA_ROW),
        row_spec(256), row_spec(512),
    ]
    if prompt:
        out_shapes += [
            jax.ShapeDtypeStruct((NSA_GROUPS, rows, NSA_DK), BF16),
            jax.ShapeDtypeStruct((rows // 256, 128, 256), BF16),
            jax.ShapeDtypeStruct((NSA_GROUPS, rows, NSA_DK), BF16),
            jax.ShapeDtypeStruct((rows // 128, 128, 128), BF16),
        ]
        out_specs += [
            pl.BlockSpec((NSA_GROUPS, tm, NSA_DK), lambda i: (0, i, 0)),
            pl.BlockSpec((tm // 256, 128, 256), lambda i: (i, 0, 0)),
            pl.BlockSpec((NSA_GROUPS, tm, NSA_DK), lambda i: (0, i, 0)),
            pl.BlockSpec((tm // 128, 128, 128), lambda i: (i, 0, 0)),
        ]
    return pl.pallas_call(
        functools.partial(_proj_body, prompt),
        grid=(rows // tm,),
        in_specs=[row_spec(D_MODEL), _full(wp.shape), _full(wuq.shape), _full(wuk.shape),
                  _full(wqr.shape), _full(gq.shape), _full(gkv.shape),
                  row_spec(LANES), row_spec(LANES), row_spec(LANES)],
        out_specs=out_specs,
        out_shape=out_shapes,
        compiler_params=_cparams(1),
        name="proj",
    )(x, wp, wuq, wuk, wqr, gq, gkv, cos, s_lo, s_hi)


def _rope_tables(pos):
    half = D_ROPE // 2
    freq = ROPE_THETA ** (-jnp.arange(half, dtype=F32) / half)
    ang = pos.astype(F32)[:, None] * freq
    cos, sin = jnp.cos(ang), jnp.sin(ang)
    z = jnp.zeros((pos.shape[0], LANES - D_ROPE), F32)
    zh = jnp.zeros_like(sin)
    return (jnp.concatenate([cos, cos, z], 1), jnp.concatenate([-sin, zh, z], 1),
            jnp.concatenate([zh, sin, z], 1))


def _proj_weights(w_in, g_q, w_uq, w_qr, g_kv, w_uk):
    offs = np.cumsum((0,) + IN_WIDTHS)
    col = lambda i: w_in[:, offs[i]:offs[i + 1]]
    pad = lambda a, w: jnp.pad(a, ((0, 0), (0, w - a.shape[1])))
    ng = col(2).reshape(D_MODEL, NSA_HEADS, 3).transpose(0, 2, 1).reshape(D_MODEL, 24)
    wp = jnp.concatenate([col(0), col(1), pad(ng, 128), col(3), col(4), pad(col(5), 128), col(6)], 1)
    wuq = w_uq.reshape(Q_LORA, MLA_HEADS * D_NOPE)
    eye = jnp.eye(MLA_HEADS, dtype=F32)
    wuk = jnp.einsum('rhd,hg->hdgr', w_uk, eye).reshape(MLA_HEADS * D_NOPE, MLA_HEADS * KV_LORA)
    wqr = jnp.pad(w_qr, ((0, 0), (0, 0), (0, LANES - D_ROPE))).reshape(Q_LORA, MLA_HEADS * LANES)
    return (wp.astype(BF16), wuq.astype(BF16), wuk.astype(BF16), wqr.astype(BF16),
            g_q.reshape(1, Q_LORA), g_kv.reshape(1, KV_LORA))


def _compress_weights(cmp_pos, cmp_w1, cmp_w2):
    eye = jnp.eye(NSA_GROUPS, dtype=F32)
    w1r = cmp_w1.reshape(2, CMP_BLOCK, NSA_DK, CMP_HIDDEN)
    w1 = jnp.einsum('ktdc,gG->ktgdGc', w1r, eye).reshape(2, CMP_BLOCK, 128, 256)
    w2 = jnp.einsum('kcd,gG->kgcGd', cmp_w2, eye).reshape(2, 256, 128)
    pos = jnp.concatenate([cmp_pos, cmp_pos], axis=-1)
    return pos, w1.astype(BF16), w2.astype(BF16)


def _compress_rows(x_ref, kv, nblk, pos_ref, w1_ref, w2_ref, transposed=False):
    acc = jnp.zeros((nblk, 256), F32)
    for t in range(CMP_BLOCK):
        xt = x_ref[pl.ds(t, nblk, stride=CMP_BLOCK), :]
        acc = acc + _dot((xt + pos_ref[kv, t:t + 1, :]).astype(BF16), w1_ref[kv, t])
    h = (acc * jax.nn.sigmoid(acc)).astype(BF16)
    if transposed:
        return _dot_nt(w2_ref[kv], h)
    return _dot(h, w2_ref[kv])


def _compress_prompt_body(xk_ref, xv_ref, pos_ref, w1_ref, w2_ref, w2t_ref, kc_ref, vct_ref):
    nblk = xk_ref.shape[0] // CMP_BLOCK
    kcb = _compress_rows(xk_ref, 0, nblk, pos_ref, w1_ref, w2_ref).astype(BF16)
    vt = _compress_rows(xv_ref, 1, nblk, pos_ref, w1_ref, w2t_ref, transposed=True).astype(BF16)
    for g in range(NSA_GROUPS):
        kc_ref[g] = kcb[:, g * NSA_DK:(g + 1) * NSA_DK]
        vct_ref[g] = vt[g * NSA_DK:(g + 1) * NSA_DK, :]


def _compress_prompt(nkv4, cw):
    t = nkv4.shape[0]
    rows = min(t, 4096)
    nblk, nc = rows // CMP_BLOCK, t // CMP_BLOCK
    pos, w1, w2 = cw
    w2t = w2.transpose(0, 2, 1)
    return pl.pallas_call(
        _compress_prompt_body,
        grid=(t // rows,),
        in_specs=[pl.BlockSpec((rows, 128), lambda i: (i, 0)), pl.BlockSpec((rows, 128), lambda i: (i, 1)),
                  _full(pos.shape), _full(w1.shape), _full(w2.shape), _full(w2t.shape)],
        out_specs=[pl.BlockSpec((NSA_GROUPS, nblk, NSA_DK), lambda i: (0, i, 0)),
                   pl.BlockSpec((NSA_GROUPS, NSA_DK, nblk), lambda i: (0, 0, i))],
        out_shape=[jax.ShapeDtypeStruct((NSA_GROUPS, nc, NSA_DK), BF16),
                   jax.ShapeDtypeStruct((NSA_GROUPS, NSA_DK, nc), BF16)],
        compiler_params=_cparams(1),
        name="compress_prompt",
    )(nkv4, nkv4, pos, w1, w2, w2t)


def _t5_bucket(dist):
    n = jnp.maximum(dist, 0)
    max_exact = N_BUCKETS // 2
    nf = jnp.maximum(n, 1).astype(F32)
    large = max_exact + (jnp.log(nf / max_exact) / math.log(MAX_DISTANCE / max_exact)
                         * (N_BUCKETS - max_exact)).astype(jnp.int32)
    large = jnp.minimum(large, N_BUCKETS - 1)
    return jnp.where(n < max_exact, n, large)


FAR_DIST = 129


def _rel_bias(rel_bias, dist):
    b = rel_bias[_t5_bucket(dist)] - rel_bias[N_BUCKETS - 1]
    b = jnp.where((dist >= 0)[..., None], b, NEG)
    return jnp.moveaxis(b, -1, 0)


def _lanes_hq(b):
    k = b.shape[1]
    return b.reshape(NSA_GROUPS, NSA_HPG, k, 128).transpose(0, 2, 1, 3).reshape(NSA_GROUPS, k, 512)


def _prompt_bias_tables(rel_bias):
    q = jnp.arange(128)[None, :]
    k = jnp.arange(128)[:, None]
    zero = jnp.zeros((NSA_GROUPS, 128, 512), F32)
    sub = _lanes_hq(_rel_bias(rel_bias, 128 + q - k))
    diag = _lanes_hq(_rel_bias(rel_bias, q - k))
    neg = jnp.full((NSA_GROUPS, 128, 512), NEG, F32)
    anti = _lanes_hq(jnp.broadcast_to(jnp.where(k >= q, 0.0, NEG)[None], (8, 128, 128)))
    near = jnp.stack([zero, sub, diag, neg, anti], axis=1)
    r = jnp.arange(16)[:, None]
    cmpw = jnp.stack([_lanes_hq(_rel_bias(rel_bias, q - CMP_BLOCK * (r - off) - (CMP_BLOCK - 1)))
                      for off in (4, 8, 0)], axis=1)
    return near, cmpw


T_ZERO, T_SUB, T_DIAG, T_NEG, T_ANTI = range(5)


def _softmax_step(s, m_ref, l_ref, acc_ref, vt):
    m_old = m_ref[...]
    m_new = jnp.maximum(m_old, jnp.max(s, axis=0, keepdims=True))
    alpha = jnp.exp(m_old - m_new)
    e = jnp.exp(s - m_new)
    l_ref[...] = alpha * l_ref[...] + jnp.sum(e, axis=0, keepdims=True)
    acc_ref[...] = alpha * acc_ref[...] + _dot(vt, e.astype(BF16))
    m_ref[...] = m_new


def _topk_rows(score, n_sel):
    nb = score.shape[0]
    blk = lax.broadcasted_iota(jnp.int32, score.shape, 0)
    sel = jnp.zeros(score.shape, F32)
    picks = []
    for _ in range(n_sel):
        m = jnp.max(score, axis=0, keepdims=True)
        j = jnp.min(jnp.where(score == m, blk, nb), axis=0, keepdims=True)
        hit = blk == j
        sel = jnp.where(hit, 1.0, sel)
        score = jnp.where(hit, -2.0, score)
        picks.append(j)
    return sel, picks


def _nsa_prompt_body(q_ref, gate_ref, kc_ref, vct_ref, kslc_ref, vslct_ref, kwin_ref, vwint_ref,
                     near_ref, cmpw_ref, o_ref, s_ref, imp_ref, sel_ref, m_ref, l_ref, acc_ref):
    i = pl.program_id(0)
    nc = kc_ref.shape[1]
    nb = nc // 2
    gate_t = gate_ref[...].T
    lane_q = lax.broadcasted_iota(jnp.int32, (1, 128), 1)
    cur = 2 * i + (lane_q >= SEL_BLOCK).astype(jnp.int32)
    odd = i % 2
    w0 = pl.multiple_of(jnp.where(i == 0, 0, jnp.where(odd == 1, 4 * i - 4, 4 * i - 8)), 8)
    var = jnp.where(i == 0, 2, jnp.where(odd == 1, 0, 1))
    jl = i // 2
    even = 1 - odd

    def reset():
        m_ref[...] = jnp.full(m_ref.shape, M_INIT, F32)
        l_ref[...] = jnp.zeros(l_ref.shape, F32)
        acc_ref[...] = jnp.zeros(acc_ref.shape, F32)

    def gate_row(g, b):
        return jnp.concatenate([gate_t[b * 8 + 4 * g + h:b * 8 + 4 * g + h + 1, :]
                                for h in range(NSA_HPG)], axis=1)

    s_ref[pl.ds(nc, 16), :] = jnp.zeros((16, 512), F32)
    for g in range(NSA_GROUPS):
        qg = q_ref[4 * g:4 * g + 4].reshape(4 * 128, NSA_DK)
        s_ref[pl.ds(0, nc), :] = _dot_nt(kc_ref[g], qg)
        s_ref[pl.ds(w0, 16), :] = s_ref[pl.ds(w0, 16), :] + cmpw_ref[g, var]
        row_n = lax.broadcasted_iota(jnp.int32, (nc, 1), 0)
        s = jnp.where(row_n <= 4 * i + 3, s_ref[pl.ds(0, nc), :], NEG)
        m = jnp.maximum(jnp.max(s, axis=0, keepdims=True), M_INIT)
        e = jnp.exp(s - m)
        d = jnp.sum(e, axis=0, keepdims=True)
        p = e / jnp.where(d > 0.0, d, 1.0)
        out = gate_row(g, 0) * _dot(vct_ref[g], p.astype(BF16))
        imp_ref[...] = p[:, 0:128] + p[:, 128:256] + p[:, 256:384] + p[:, 384:512]
        imp = imp_ref[pl.ds(0, nb, stride=2), :] + imp_ref[pl.ds(1, nb, stride=2), :]
        blk = lax.broadcasted_iota(jnp.int32, (nb, 128), 0)
        valid = blk <= cur
        forced = valid & ((blk == 0) | (blk == cur) | (blk == cur - 1))
        score = jnp.where(forced, NSA_HPG + 1.0, jnp.where(valid, imp, -1.0))
        sel_ref[...], _ = _topk_rows(score, min(N_SEL, nb))

        def sel_chunk(j, top, bot):
            s = _dot_nt(kslc_ref[g, pl.ds(pl.multiple_of(j * 256, 256), 256), :], qg)
            parts = []
            for b in range(4):
                row = sel_ref[pl.ds(4 * j + b, 1), :]
                mask = jnp.concatenate([row] * NSA_HPG, axis=1) > 0.5
                sb = s[b * SEL_BLOCK:(b + 1) * SEL_BLOCK]
                if top is not None:
                    tab = near_ref[g, top if b < 2 else bot]
                    sb = sb + tab[(b % 2) * SEL_BLOCK:(b % 2 + 1) * SEL_BLOCK]
                parts.append(jnp.where(mask, sb, NEG))
            _softmax_step(jnp.concatenate(parts, axis=0), m_ref, l_ref, acc_ref,
                          vslct_ref[j, g * NSA_DK:(g + 1) * NSA_DK, :])

        reset()

        def far_body(j, carry):
            sel_chunk(j, None, None)
            return carry

        lax.fori_loop(0, jnp.maximum(jl - 1, 0), far_body, 0)

        @pl.when(jl >= 1)
        def _():
            sel_chunk(jl - 1, T_ZERO, jnp.where(even == 1, T_SUB, T_ZERO))

        sel_chunk(jl, jnp.where(even == 1, T_DIAG, T_SUB), jnp.where(even == 1, T_NEG, T_DIAG))
        out = out + (gate_row(g, 1) / l_ref[...]) * acc_ref[...]

        reset()
        for back, tab in ((4, T_ANTI), (3, None), (2, None), (1, T_SUB), (0, T_DIAG)):
            @pl.when(i >= back)
            def _(back=back, tab=tab):
                j = i - back
                s = _dot_nt(kwin_ref[g, pl.ds(pl.multiple_of(j * 128, 128), 128), :], qg)
                if tab is not None:
                    s = s + near_ref[g, tab]
                _softmax_step(s, m_ref, l_ref, acc_ref, vwint_ref[j, g * NSA_DK:(g + 1) * NSA_DK, :])
        out = out + (gate_row(g, 2) / l_ref[...]) * acc_ref[...]
        for h in range(NSA_HPG):
            col = (4 * g + h) * NSA_DK
            o_ref[:, col:col + NSA_DK] = out[:, h * 128:(h + 1) * 128].T.astype(BF16)


def _nsa_prompt(qn, gate, kc, vct, kslc, vslct, kwin, vwint, near, cmpw):
    t = qn.shape[1]
    nc = kc.shape[1]
    return pl.pallas_call(
        _nsa_prompt_body,
        grid=(t // 128,),
        in_specs=[pl.BlockSpec((NSA_HEADS, 128, NSA_DK), lambda i: (0, i, 0)),
                  pl.BlockSpec((128, LANES), lambda i: (i, 0)),
                  _full(kc.shape), _full(vct.shape), _full(kslc.shape), _full(vslct.shape),
                  _full(kwin.shape), _full(vwint.shape), _full(near.shape), _full(cmpw.shape)],
        out_specs=pl.BlockSpec((128, 512), lambda i: (i, 0)),
        out_shape=jax.ShapeDtypeStruct((t, 512), BF16),
        scratch_shapes=[pltpu.VMEM((nc + 16, 512), F32), pltpu.VMEM((nc, 128), F32),
                        pltpu.VMEM((nc // 2, 128), F32), pltpu.VMEM((1, 512), F32),
                        pltpu.VMEM((1, 512), F32), pltpu.VMEM((NSA_DK, 512), F32)],
        compiler_params=_cparams(1),
        name="nsa_prompt",
    )(qn, gate, kc, vct, kslc, vslct, kwin, vwint, near, cmpw)


MLA_TQ = 128
MLA_KC = 512
MLA_SCALE = (D_NOPE + D_ROPE) ** -0.5


def _mla_prompt_body(q_ref, k_ref, o_ref):
    i = pl.program_id(0)
    q = q_ref[...]
    qs = jnp.concatenate([q[:, h * 256:(h + 1) * 256] for h in range(MLA_HEADS)], axis=0)
    rows = MLA_HEADS * MLA_TQ
    last = (i * MLA_TQ) // MLA_KC

    def step(j, carry, causal):
        m_old, l_old, acc = carry
        kc = k_ref[pl.ds(pl.multiple_of(j * MLA_KC, MLA_KC), MLA_KC), :]
        s = _dot_nt(qs, kc) * MLA_SCALE
        if causal:
            q_pos = i * MLA_TQ + lax.broadcasted_iota(jnp.int32, (rows, 1), 0) % MLA_TQ
            k_pos = j * MLA_KC + lax.broadcasted_iota(jnp.int32, (1, MLA_KC), 1)
            s = jnp.where(k_pos <= q_pos, s, NEG)
        m_new = jnp.maximum(m_old, jnp.max(s, axis=1, keepdims=True))
        alpha = jnp.exp(m_old - m_new)
        e = jnp.exp(s - m_new)
        l_new = alpha * l_old + jnp.sum(e, axis=1, keepdims=True)
        acc = alpha * acc + _dot(e.astype(BF16), kc[:, :KV_LORA])
        return m_new, l_new, acc

    init = (jnp.full((rows, 1), M_INIT, F32), jnp.zeros((rows, 1), F32), jnp.zeros((rows, KV_LORA), F32))
    carry = lax.fori_loop(0, last, lambda j, c: step(j, c, False), init)
    _, l, acc = step(last, carry, True)
    o = acc / l
    for h in range(MLA_HEADS):
        o_ref[:, h * KV_LORA:(h + 1) * KV_LORA] = o[h * MLA_TQ:(h + 1) * MLA_TQ].astype(BF16)


def _mla_prompt(qmla, kmla):
    t = qmla.shape[0]
    return pl.pallas_call(
        _mla_prompt_body,
        grid=(t // MLA_TQ,),
        in_specs=[pl.BlockSpec((MLA_TQ, 2048), lambda i: (i, 0)), _full(kmla.shape)],
        out_specs=pl.BlockSpec((MLA_TQ, MLA_HEADS * KV_LORA), lambda i: (i, 0)),
        out_shape=jax.ShapeDtypeStruct((t, MLA_HEADS * KV_LORA), BF16),
        compiler_params=_cparams(1),
        name="mla_prompt",
    )(qmla, kmla)


def _mem_kv_body(x_ref, w_ref, o_ref, ob_ref):
    kv = _dot(x_ref[...].astype(BF16), w_ref[...])
    o_ref[...] = kv
    ob_ref[...] = kv.astype(BF16)


def _mem_kv(mem, w):
    shp = (MEM_TOKENS, 2 * MEM_HEADS * MEM_DH)
    return pl.pallas_call(
        _mem_kv_body,
        in_specs=[_full(mem.shape), _full(w.shape)],
        out_specs=[_full(shp), _full(shp)],
        out_shape=[jax.ShapeDtypeStruct(shp, F32), jax.ShapeDtypeStruct(shp, BF16)],
        grid=(1,),
        compiler_params=_cparams(1),
        name="mem_kv",
    )(mem, w)


def _mem_attn_body(q_ref, kv_ref, o_ref):
    for h in range(MEM_HEADS):
        sl = slice(h * MEM_DH, (h + 1) * MEM_DH)
        k = kv_ref[:, sl]
        v = kv_ref[:, MEM_HEADS * MEM_DH + h * MEM_DH:MEM_HEADS * MEM_DH + (h + 1) * MEM_DH]
        s = _dot_nt(q_ref[:, sl], k) * (MEM_DH ** -0.5)
        e = jnp.exp(s - jnp.max(s, axis=1, keepdims=True))
        p = e / jnp.sum(e, axis=1, keepdims=True)
        o_ref[:, sl] = _dot(p.astype(BF16), v).astype(BF16)


def _mem_attn(mq, kvb, tm):
    t = mq.shape[0]
    return pl.pallas_call(
        _mem_attn_body,
        grid=(t // tm,),
        in_specs=[pl.BlockSpec((tm, 512), lambda i: (i, 0)), _full(kvb.shape)],
        out_specs=pl.BlockSpec((tm, 512), lambda i: (i, 0)),
        out_shape=jax.ShapeDtypeStruct((t, 512), BF16),
        compiler_params=_cparams(1),
        name="mem_attn",
    )(mq, kvb)


def _merge_weights(w_in, w_uv, w_br, w_o):
    wmg = w_in[:, sum(IN_WIDTHS[:-1]):]
    eye = jnp.eye(MLA_HEADS, dtype=F32)
    wuv = jnp.einsum('rhd,hg->hrgd', w_uv, eye).reshape(MLA_HEADS * KV_LORA, MLA_HEADS * D_V)
    return wmg.astype(BF16), wuv.astype(BF16), w_br.astype(BF16), w_o.astype(BF16)


def _merge_body(x_ref, onsa_ref, olat_ref, omem_ref, wmg_ref, wuv_ref, wbr_ref, wo_ref, g_ref, b_ref,
                o_ref):
    x = x_ref[...]
    xb = x.astype(BF16)
    v_mla = _dot(olat_ref[...], wuv_ref[...]).astype(BF16)
    tot = jnp.zeros(x.shape, F32)
    for b, br in enumerate((onsa_ref[...], v_mla, omem_ref[...])):
        gate = jax.nn.sigmoid(_dot(xb, wmg_ref[:, b * D_MODEL:(b + 1) * D_MODEL]))
        tot = tot + gate * _dot(br, wbr_ref[b])
    mix = _dot(tot.astype(BF16), wo_ref[...])
    o_ref[...] = _layer_norm(ALPHA * x + mix, g_ref[...], b_ref[...])


def _merge(x, onsa, olat, omem, mw, g, b, tm):
    rows = x.shape[0]
    wmg, wuv, wbr, wo = mw
    row_spec = lambda w: pl.BlockSpec((tm, w), lambda i: (i, 0))
    return pl.pallas_call(
        _merge_body,
        grid=(rows // tm,),
        in_specs=[row_spec(D_MODEL), row_spec(512), row_spec(1024), row_spec(512),
                  _full(wmg.shape), _full(wuv.shape), _full(wbr.shape), _full(wo.shape),
                  _full(g.shape), _full(b.shape)],
        out_specs=row_spec(D_MODEL),
        out_shape=jax.ShapeDtypeStruct((rows, D_MODEL), F32),
        compiler_params=_cparams(1),
        name="merge",
    )(x, onsa, olat, omem, mw[0], wuv, wbr, wo, g, b)


def _prompt_path(x_prompt, mem_prompt, rel_bias, ln_g, ln_b, ffn_w, pw, cw, mw, mem_w_kv):
    t = x_prompt.shape[1]
    tm = 512
    ln = lambda k: (ln_g[0, k][None], ln_b[0, k][None])
    x1 = _ffn(x_prompt[0], *ffn_w[0], *ln(0), tm)
    (qn, nkv4, win, gate, qmla, row, kmla, mq, kslc, vslct, kwin, vwint) = _proj(
        x1, pw, _rope_tables(jnp.arange(t)), tm, True)
    kc, vct = _compress_prompt(nkv4, cw)
    near, cmpw = _prompt_bias_tables(rel_bias)
    o_nsa = _nsa_prompt(qn, gate, kc, vct, kslc, vslct, kwin, vwint, near, cmpw)
    o_lat = _mla_prompt(qmla, kmla)
    mem_kv, mem_kvb = _mem_kv(mem_prompt[0], mem_w_kv[0].reshape(D_MODEL, -1).astype(BF16))
    o_mem = _mem_attn(mq, mem_kvb, tm)
    x2 = _merge(x1, o_nsa, o_lat, o_mem, mw, *ln(1), tm)
    y = _ffn(x2, *ffn_w[1], *ln(2), tm)
    wb = min(WINDOW, t)
    return (y[None],
            nkv4.reshape(1, 1, t, 4, NSA_GROUPS, NSA_DK),
            row.reshape(1, 1, t, MLA_ROW),
            win[t - wb:].reshape(1, 1, wb, 2, NSA_GROUPS, NSA_DK),
            mem_kv.reshape(1, 1, MEM_TOKENS, 2, MEM_HEADS, MEM_DH))


def _own_group_lanes(q):
    q2 = jnp.concatenate([q, q], axis=1)
    head = lax.broadcasted_iota(jnp.int32, q2.shape, 0)
    lane = lax.broadcasted_iota(jnp.int32, q2.shape, 1)
    return jnp.where(head // NSA_HPG == lane // NSA_DK, q2, jnp.zeros_like(q2))


def _bf(x):
    return x.astype(BF16)


def _new_key_score(qm, k_new):
    return jnp.sum(qm.astype(F32) * _bf(k_new).astype(F32), axis=1, keepdims=True)


def _paged_copies(pt_ref, seq, src_ref, lane0, lanes, dst_ref, sem, first_page, n_pages):
    def copy(p):
        return pltpu.make_async_copy(
            src_ref.at[pt_ref[seq, first_page + p], :, pl.ds(lane0, lanes)],
            dst_ref.at[pl.ds(pl.multiple_of(p * PAGE, PAGE), PAGE), :], sem)

    def start():
        lax.fori_loop(0, n_pages, lambda p, c: (copy(p).start(), c)[1], 0)

    def wait():
        lax.fori_loop(0, n_pages, lambda p, c: (copy(p).wait(), c)[1], 0)

    return start, wait


def _s_cmp_win_body(pt_ref, cache_ref, q_ref, g_ref, st_ref, wnew_ref, pos_ref, w1_ref, w2_ref,
                    bc_ref, bw_ref, bn_ref, o8_ref, imp_ref, buf, sem):
    b = pl.program_id(0)
    n_pages = buf.shape[2] // PAGE
    nc = buf.shape[2] // CMP_BLOCK
    slot = b % 2

    def copies(seq, sl, which):
        for kv in range(2):
            _paged_copies(pt_ref, seq, cache_ref, kv * 128, 128, buf.at[sl, kv], sem.at[sl], 0,
                          n_pages)[which]()

    @pl.when(b == 0)
    def _():
        copies(0, 0, 0)

    @pl.when(b + 1 < pl.num_programs(0))
    def _():
        copies(b + 1, 1 - slot, 0)

    copies(b, slot, 1)
    kc = _compress_rows(buf.at[slot, 0], 0, nc, pos_ref, w1_ref, w2_ref)
    vc = _compress_rows(buf.at[slot, 1], 1, nc, pos_ref, w1_ref, w2_ref)
    qm = _own_group_lanes(q_ref[0])
    gate = g_ref[0]
    s = _dot_nt(qm, _bf(kc)) + bc_ref[...]
    e = jnp.exp(s - jnp.max(s, axis=1, keepdims=True))
    p = e / jnp.sum(e, axis=1, keepdims=True)
    out = gate[:, 0:1] * _dot(_bf(p), _bf(vc))
    for g in range(NSA_GROUPS):
        imp_ref[0, g:g + 1, :] = jnp.sum(p[NSA_HPG * g:NSA_HPG * (g + 1)], axis=0, keepdims=True)
    st = st_ref[0]
    wnew = wnew_ref[0]
    s = _dot_nt(qm, _bf(st[:, :128])) + bw_ref[...]
    s_new = _new_key_score(qm, wnew[:, :128]) + bn_ref[...]
    m = jnp.maximum(jnp.max(s, axis=1, keepdims=True), s_new)
    e = jnp.exp(s - m)
    e_new = jnp.exp(s_new - m)
    l = jnp.sum(e, axis=1, keepdims=True) + e_new
    acc = _dot(_bf(e), _bf(st[:, 128:])) + _bf(e_new).astype(F32) * _bf(wnew[:, 128:]).astype(F32)
    o8_ref[0] = out + (gate[:, 2:3] / l) * acc


def _s_cmp_win(page_table, cache, q3, g3, state, wnew, cw, bc, bw, bn):
    db, n_pages = page_table.shape
    past = n_pages * PAGE
    nc = past // CMP_BLOCK
    pos, w1, w2 = cw
    wb = state.shape[1]
    one = lambda *tail: pl.BlockSpec((1,) + tail, lambda b, pt: (b,) + (0,) * len(tail))
    fullp = lambda shape: pl.BlockSpec(shape, lambda b, pt: (0,) * len(shape))
    return pl.pallas_call(
        _s_cmp_win_body,
        grid_spec=pltpu.PrefetchScalarGridSpec(
            num_scalar_prefetch=1,
            grid=(db,),
            in_specs=[pl.BlockSpec(memory_space=pl.ANY), one(NSA_HEADS, NSA_DK), one(NSA_HEADS, 3),
                      one(wb, 256), one(1, 256), fullp(pos.shape), fullp(w1.shape), fullp(w2.shape),
                      fullp(bc.shape), fullp(bw.shape), fullp(bn.shape)],
            out_specs=[one(NSA_HEADS, 128), one(NSA_GROUPS, nc)],
            scratch_shapes=[pltpu.VMEM((2, 2, past, 128), F32), pltpu.SemaphoreType.DMA((2,))],
        ),
        out_shape=[jax.ShapeDtypeStruct((db, NSA_HEADS, 128), F32),
                   jax.ShapeDtypeStruct((db, NSA_GROUPS, nc), F32)],
        compiler_params=_cparams(1),
        name="sample_cmp_win",
    )(page_table, cache, q3, g3, state, wnew, pos, w1, w2, bc, bw, bn)


def _s_topk_body(n_blk, imp_ref, idx_ref, t_ref):
    db, nc = imp_ref.shape[0], imp_ref.shape[2]
    nbp = nc // 2
    rows = t_ref.shape[0] // 2
    cur = n_blk - 1
    for g in range(NSA_GROUPS):
        t_ref[pl.ds(0, nc), :] = imp_ref[:, g, :].T
        t_ref[pl.ds(nc, t_ref.shape[0] - nc), :] = jnp.zeros((t_ref.shape[0] - nc, db), F32)
        imp = t_ref[pl.ds(0, rows, stride=2), :] + t_ref[pl.ds(1, rows, stride=2), :]
        blk = lax.broadcasted_iota(jnp.int32, (rows, db), 0)
        forced = (blk == 0) | (blk == cur) | (blk == cur - 1)
        score = jnp.where(blk >= n_blk, -2.0, jnp.where(forced, NSA_HPG + 1.0, imp))
        _, picks = _topk_rows(score, min(N_SEL, n_blk))
        idx_ref[g] = jnp.concatenate(picks, axis=0)


def _s_topk(imp, n_blk):
    db, _, nc = imp.shape
    rows = -(-n_blk // 8) * 8
    n_sel = min(N_SEL, n_blk)
    return pl.pallas_call(
        functools.partial(_s_topk_body, n_blk),
        grid=(1,),
        in_specs=[_full(imp.shape)],
        out_specs=_full((NSA_GROUPS, n_sel, db)),
        out_shape=jax.ShapeDtypeStruct((NSA_GROUPS, n_sel, db), jnp.int32),
        scratch_shapes=[pltpu.VMEM((2 * rows, db), F32)],
        compiler_params=_cparams(1),
        name="sample_topk",
    )(imp)


def _s_select_body(n_sel, pt_ref, idx_ref, cache_ref, q_ref, g_ref, new_ref, o8in_ref, ta_ref, tb_ref,
                   bn_ref, o8_ref, buf, sem):
    b = pl.program_id(0)
    nbp = pt_ref.shape[1] * (PAGE // SEL_BLOCK)
    slot = b % 2
    n_dma = NSA_GROUPS * n_sel

    def copy(seq, sl, n):
        blk = jnp.minimum(idx_ref[n, seq], nbp - 1)
        page = pt_ref[seq, blk // 2]
        return pltpu.make_async_copy(
            cache_ref.at[page, pl.ds(pl.multiple_of((blk % 2) * SEL_BLOCK, SEL_BLOCK), SEL_BLOCK),
                         pl.ds(256, 256)],
            buf.at[sl, n], sem.at[sl])

    def start(seq, sl):
        lax.fori_loop(0, n_dma, lambda n, c: (copy(seq, sl, n).start(), c)[1], 0)

    @pl.when(b == 0)
    def _():
        start(0, 0)

    @pl.when(b + 1 < pl.num_programs(0))
    def _():
        start(b + 1, 1 - slot)

    lax.fori_loop(0, n_dma, lambda n, c: (copy(b, slot, n).wait(), c)[1], 0)
    qm = _own_group_lanes(q_ref[0])
    gate = g_ref[0]
    new = new_ref[0]
    head = lax.broadcasted_iota(jnp.int32, (NSA_HEADS, 1), 0)
    s_new = _new_key_score(qm, new[:, 256:384]) + bn_ref[...]
    v_new = _bf(new[:, 384:512]).astype(F32)
    out = o8in_ref[0]
    for g in range(NSA_GROUPS):
        kv = buf[slot, pl.ds(g * n_sel, n_sel)].reshape(n_sel * SEL_BLOCK, 256)
        pieces = []
        for r in range(n_sel):
            blk = idx_ref[g * n_sel + r, b]
            wa = jnp.where(blk == nbp - 1, 1.0, 0.0)
            wb = jnp.where(blk == nbp - 2, 1.0, 0.0)
            wm = jnp.where(blk >= nbp, NEG, 0.0)
            pieces.append(wa * ta_ref[...] + wb * tb_ref[...] + wm)
        s = _dot_nt(qm, _bf(kv[:, :128])) + jnp.concatenate(pieces, axis=1)
        m = jnp.maximum(jnp.max(s, axis=1, keepdims=True), s_new)
        e = jnp.exp(s - m)
        e_new = jnp.exp(s_new - m)
        l = jnp.sum(e, axis=1, keepdims=True) + e_new
        acc = _dot(_bf(e), _bf(kv[:, 128:])) + _bf(e_new).astype(F32) * v_new
        out = out + jnp.where(head // NSA_HPG == g, (gate[:, 1:2] / l) * acc, 0.0)
    o8_ref[0] = out


def _s_select(page_table, idx2, cache, q3, g3, new4, o8, ta, tb, bn):
    db = page_table.shape[0]
    n_sel = idx2.shape[0] // NSA_GROUPS
    one = lambda *tail: pl.BlockSpec((1,) + tail, lambda b, pt, ix: (b,) + (0,) * len(tail))
    fullp = lambda shape: pl.BlockSpec(shape, lambda b, pt, ix: (0,) * len(shape))
    return pl.pallas_call(
        functools.partial(_s_select_body, n_sel),
        grid_spec=pltpu.PrefetchScalarGridSpec(
            num_scalar_prefetch=2,
            grid=(db,),
            in_specs=[pl.BlockSpec(memory_space=pl.ANY), one(NSA_HEADS, NSA_DK), one(NSA_HEADS, 3),
                      one(1, 512), one(NSA_HEADS, 128), fullp(ta.shape), fullp(tb.shape), fullp(bn.shape)],
            out_specs=one(NSA_HEADS, 128),
            scratch_shapes=[pltpu.VMEM((2, NSA_GROUPS * n_sel, SEL_BLOCK, 256), F32),
                            pltpu.SemaphoreType.DMA((2,))],
        ),
        out_shape=jax.ShapeDtypeStruct((db, NSA_HEADS, 128), F32),
        compiler_params=_cparams(1),
        name="sample_select",
    )(page_table, idx2, cache, q3, g3, new4, o8, ta, tb, bn)


MLA_S_PAGES = 32


def _s_mla_body(pt_ref, cache_ref, q_ref, knew_ref, o_ref, buf, sem, m_ref, l_ref, acc_ref):
    b, c = pl.program_id(0), pl.program_id(1)
    ncb = pl.num_programs(1)
    n_pages = buf.shape[1] // PAGE
    n = b * ncb + c
    slot = n % 2

    def copies(step, sl):
        return _paged_copies(pt_ref, step // ncb, cache_ref, 0, MLA_ROW, buf.at[sl], sem.at[sl],
                             (step % ncb) * n_pages, n_pages)

    @pl.when(n == 0)
    def _():
        copies(0, 0)[0]()

    @pl.when(n + 1 < pl.num_programs(0) * ncb)
    def _():
        copies(n + 1, 1 - slot)[0]()

    @pl.when(c == 0)
    def _():
        m_ref[...] = jnp.full(m_ref.shape, M_INIT, F32)
        l_ref[...] = jnp.zeros(l_ref.shape, F32)
        acc_ref[...] = jnp.zeros(acc_ref.shape, F32)

    copies(n, slot)[1]()
    q = q_ref[0][:, :MLA_ROW]
    kb = _bf(buf[slot])
    s = _dot_nt(q, kb) * MLA_SCALE
    m_old = m_ref[...]
    m_new = jnp.maximum(m_old, jnp.max(s, axis=1, keepdims=True))
    alpha = jnp.exp(m_old - m_new)
    e = jnp.exp(s - m_new)
    l_ref[...] = alpha * l_ref[...] + jnp.sum(e, axis=1, keepdims=True)
    acc_ref[...] = alpha * acc_ref[...] + _dot(_bf(e), kb[:, :KV_LORA])
    m_ref[...] = m_new

    @pl.when(c == ncb - 1)
    def _():
        k_new = knew_ref[0]
        s_new = jnp.sum(q_ref[0].astype(F32) * k_new.astype(F32), axis=1, keepdims=True) * MLA_SCALE
        m_old = m_ref[...]
        m_new = jnp.maximum(m_old, s_new)
        alpha = jnp.exp(m_old - m_new)
        e_new = jnp.exp(s_new - m_new)
        l = alpha * l_ref[...] + e_new
        acc = alpha * acc_ref[...] + _bf(e_new).astype(F32) * k_new[:, :KV_LORA].astype(F32)
        o_ref[0] = _bf(acc / l)


def _s_mla(page_table, cache, q3, knew):
    db, n_pages = page_table.shape
    step_pages = min(MLA_S_PAGES, n_pages)
    ncb = n_pages // step_pages
    one = lambda *tail: pl.BlockSpec((1,) + tail, lambda b, c, pt: (b,) + (0,) * len(tail))
    return pl.pallas_call(
        _s_mla_body,
        grid_spec=pltpu.PrefetchScalarGridSpec(
            num_scalar_prefetch=1,
            grid=(db, ncb),
            in_specs=[pl.BlockSpec(memory_space=pl.ANY), one(MLA_HEADS, 256), one(1, 256)],
            out_specs=one(MLA_HEADS, KV_LORA),
            scratch_shapes=[pltpu.VMEM((2, step_pages * PAGE, MLA_ROW), F32),
                            pltpu.SemaphoreType.DMA((2,)), pltpu.VMEM((MLA_HEADS, 1), F32),
                            pltpu.VMEM((MLA_HEADS, 1), F32), pltpu.VMEM((MLA_HEADS, KV_LORA), F32)],
        ),
        out_shape=jax.ShapeDtypeStruct((db, MLA_HEADS, KV_LORA), BF16),
        compiler_params=_cparams(2),
        name="sample_mla",
    )(page_table, cache, q3, knew)


def _s_mem_body(q_ref, kv_ref, o_ref):
    q = q_ref[0]
    head = lax.broadcasted_iota(jnp.int32, (MEM_HEADS, 1), 0)
    out = jnp.zeros((MEM_HEADS, MEM_DH), F32)
    for h in range(MEM_HEADS):
        k = _bf(kv_ref[0, :, h * MEM_DH:(h + 1) * MEM_DH])
        v = _bf(kv_ref[0, :, (MEM_HEADS + h) * MEM_DH:(MEM_HEADS + h + 1) * MEM_DH])
        s = _dot_nt(q, k) * (MEM_DH ** -0.5)
        e = jnp.exp(s - jnp.max(s, axis=1, keepdims=True))
        p = e / jnp.sum(e, axis=1, keepdims=True)
        out = jnp.where(head == h, _dot(_bf(p), v), out)
    o_ref[0] = _bf(out)


def _s_mem(mq3, cache_mem):
    db = mq3.shape[0]
    one = lambda *tail: pl.BlockSpec((1,) + tail, lambda b: (b,) + (0,) * len(tail))
    return pl.pallas_call(
        _s_mem_body,
        grid=(db,),
        in_specs=[one(MEM_HEADS, MEM_DH), one(MEM_TOKENS, 2 * MEM_HEADS * MEM_DH)],
        out_specs=one(MEM_HEADS, MEM_DH),
        out_shape=jax.ShapeDtypeStruct((db, MEM_HEADS, MEM_DH), BF16),
        compiler_params=_cparams(1),
        name="sample_mem",
    )(mq3, cache_mem)


def _sample_path(x_sample, cache_nsa_kv, cache_mla, state_nsa_win, cache_mem_kv, page_table, rel_bias,
                 ln_g, ln_b, ffn_w, pw, cw, mw):
    db = x_sample.shape[0]
    n_pages = page_table.shape[1]
    past = n_pages * PAGE
    wb = state_nsa_win.shape[2]
    n_blk = (past + 1 + SEL_BLOCK - 1) // SEL_BLOCK
    ln = lambda k: (ln_g[0, k][None], ln_b[0, k][None])
    x1 = _ffn(x_sample[:, 0], *ffn_w[0], *ln(0), db)
    qn, nkv4, win, gate, qmla, row, kmla, mq = _proj(
        x1, pw, _rope_tables(jnp.full((db,), past, jnp.int32)), db, False)
    q3 = qn.reshape(db, NSA_HEADS, NSA_DK)
    g3 = gate[:, :24].reshape(db, 3, NSA_HEADS).transpose(0, 2, 1)
    cache = cache_nsa_kv[0].reshape(-1, PAGE, 512)
    state = state_nsa_win[0].reshape(db, wb, 256)
    bc = _rel_bias(rel_bias, past - CMP_BLOCK * jnp.arange(past // CMP_BLOCK) - (CMP_BLOCK - 1))
    bw = _rel_bias(rel_bias, wb - jnp.arange(wb))
    bn = _rel_bias(rel_bias, jnp.zeros((1,), jnp.int32))
    ta = _rel_bias(rel_bias, SEL_BLOCK - jnp.arange(SEL_BLOCK))
    tb = _rel_bias(rel_bias, 2 * SEL_BLOCK - jnp.arange(SEL_BLOCK))
    o8, imp = _s_cmp_win(page_table, cache, q3, g3, state, win.reshape(db, 1, 256), cw, bc, bw, bn)
    idx = _s_topk(imp, n_blk)
    o8 = _s_select(page_table, idx.reshape(-1, db), cache, q3, g3, nkv4.reshape(db, 1, 512), o8, ta, tb, bn)
    o_nsa = o8.reshape(db, NSA_HEADS, NSA_GROUPS, NSA_DK)[:, jnp.arange(NSA_HEADS),
                                                           jnp.arange(NSA_HEADS) // NSA_HPG]
    o_nsa = o_nsa.reshape(db, 512).astype(BF16)
    o_lat = _s_mla(page_table, cache_mla[0], qmla.reshape(db, MLA_HEADS, 256), kmla.reshape(db, 1, 256))
    o_mem = _s_mem(mq.reshape(db, MEM_HEADS, MEM_DH), cache_mem_kv[0].reshape(db, MEM_TOKENS, -1))
    x2 = _merge(x1, o_nsa, o_lat.reshape(db, -1), o_mem.reshape(db, -1), mw, *ln(1), db)
    y = _ffn(x2, *ffn_w[1], *ln(2), db)
    new_win = jnp.concatenate([state[:, 1:], win[:, None, :]], axis=1)
    return (y[:, None],
            nkv4.reshape(1, db, 1, 4, NSA_GROUPS, NSA_DK),
            row.reshape(1, db, 1, MLA_ROW),
            new_win.reshape(1, db, wb, 2, NSA_GROUPS, NSA_DK))


def kernel(x_prompt, x_sample, mem_prompt, cache_nsa_kv, cache_mla, state_nsa_win, cache_mem_kv, page_table, rel_bias, ln_g, ln_b, ffn_w1, ffn_w3, ffn_w2, w_in, nsa_cmp_pos, nsa_cmp_w1, nsa_cmp_w2, mla_g_q, mla_w_uq, mla_w_qr, mla_g_kv, mla_w_uk, mla_w_uv, mem_w_kv, w_br, w_o):
    assert ffn_w1.shape[0] == 1 and x_prompt.shape[0] == 1 and x_sample.shape[1] == 1
    ffn_w = [tuple(w[0, s].astype(BF16) for w in (ffn_w1, ffn_w3, ffn_w2)) for s in range(2)]
    pw = _proj_weights(w_in[0], mla_g_q[0], mla_w_uq[0], mla_w_qr[0], mla_g_kv[0], mla_w_uk[0])
    cw = _compress_weights(nsa_cmp_pos[0], nsa_cmp_w1[0], nsa_cmp_w2[0])
    mw = _merge_weights(w_in[0], mla_w_uv[0], w_br[0], w_o[0])
    yp, p_nsa, p_mla, p_win, p_mem = _prompt_path(x_prompt, mem_prompt, rel_bias, ln_g, ln_b, ffn_w, pw,
                                                  cw, mw, mem_w_kv)
    ys, s_nsa, s_mla, s_win = _sample_path(x_sample, cache_nsa_kv, cache_mla, state_nsa_win, cache_mem_kv,
                                           page_table, rel_bias, ln_g, ln_b, ffn_w, pw, cw, mw)
    return (yp, ys, p_nsa, p_mla, p_win, p_mem, s_nsa, s_mla, s_win)
```

```python
import functools
import math

import jax
import jax.numpy as jnp
import numpy as np
from jax import lax
from jax.experimental import pallas as pl
from jax.experimental.pallas import tpu as pltpu

F32 = jnp.float32
BF16 = jnp.bfloat16

D_MODEL = 1024
D_FF = 2816
NSA_HEADS = 8
NSA_GROUPS = 2
NSA_HPG = 4
NSA_DK = 64
CMP_BLOCK = 32
CMP_HIDDEN = 128
SEL_BLOCK = 64
N_SEL = 16
WINDOW = 512
MLA_HEADS = 8
Q_LORA = 256
KV_LORA = 128
D_NOPE = 64
D_ROPE = 32
D_V = 64
ROPE_THETA = 10000.0
MLA_ROW = KV_LORA + D_ROPE
MEM_TOKENS = 256
MEM_HEADS = 4
MEM_DH = 128
N_BRANCH = 3
BRANCH_W = 512
N_BUCKETS = 32
MAX_DISTANCE = 128
PAGE = 128
ALPHA = 2.0 ** 0.25
LN_EPS = 1e-5
RMS_EPS = 1e-6
IN_WIDTHS = (512, 768, 24, 256, 128, 32, 512, 3072)

LANES = 128
FF_CHUNK = 256
NEG = -1e30
M_INIT = -1e29
VMEM_LIMIT = 56 * 1024 * 1024


def _cparams(n_axes):
    return pltpu.CompilerParams(dimension_semantics=("arbitrary",) * n_axes,
                                vmem_limit_bytes=VMEM_LIMIT)


def _full(shape):
    n = len(shape)
    return pl.BlockSpec(shape, lambda *_: (0,) * n)


def _dot(a, b):
    return jnp.dot(a, b, preferred_element_type=F32)


def _dot_nt(a, b):
    return lax.dot_general(a, b, (((1,), (1,)), ((), ())), preferred_element_type=F32)


def _layer_norm(y, g, b):
    mu = jnp.mean(y, axis=-1, keepdims=True)
    yc = y - mu
    var = jnp.mean(yc * yc, axis=-1, keepdims=True)
    return yc * lax.rsqrt(var + LN_EPS) * g + b


def _ffn_body(x_ref, w1_ref, w3_ref, w2_ref, g_ref, b_ref, o_ref):
    x = x_ref[...]
    xb = x.astype(BF16)
    acc = jnp.zeros(x.shape, F32)
    for c in range(D_FF // FF_CHUNK):
        sl = slice(c * FF_CHUNK, (c + 1) * FF_CHUNK)
        a = _dot(xb, w1_ref[:, sl])
        b = _dot(xb, w3_ref[:, sl])
        h = (a * jax.nn.sigmoid(a) * b).astype(BF16)
        acc = acc + _dot(h, w2_ref[sl, :])
    o_ref[...] = _layer_norm(ALPHA * x + 0.5 * acc, g_ref[...], b_ref[...])


def _ffn(x, w1, w3, w2, g, b, tm):
    rows = x.shape[0]
    return pl.pallas_call(
        _ffn_body,
        grid=(rows // tm,),
        in_specs=[pl.BlockSpec((tm, D_MODEL), lambda i: (i, 0)),
                  _full(w1.shape), _full(w3.shape), _full(w2.shape), _full(g.shape), _full(b.shape)],
        out_specs=pl.BlockSpec((tm, D_MODEL), lambda i: (i, 0)),
        out_shape=jax.ShapeDtypeStruct((rows, D_MODEL), F32),
        compiler_params=_cparams(1),
        name="ffn",
    )(x, w1, w3, w2, g, b)


P_NQ, P_NKV, P_NG, P_QD, P_KVD, P_KR, P_MQ = 0, 512, 1280, 1408, 1664, 1792, 1920
P_WIDTH = 2432


def _rope(x, cos, s_lo, s_hi):
    return (x * cos + pltpu.roll(x, LANES - D_ROPE // 2, 1) * s_lo
            + pltpu.roll(x, D_ROPE // 2, 1) * s_hi)


def _rms(x, g):
    return x * lax.rsqrt(jnp.mean(x * x, axis=-1, keepdims=True) + RMS_EPS) * g


def _proj_body(prompt, x_ref, wp_ref, wuq_ref, wuk_ref, wqr_ref, gq_ref, gkv_ref,
               cos_ref, slo_ref, shi_ref, *outs):
    if prompt:
        (qn_ref, nkv4_ref, win_ref, gate_ref, qmla_ref, row_ref, kmla_ref, mq_ref,
         kslc_ref, vslct_ref, kwin_ref, vwint_ref) = outs
    else:
        qn_ref, nkv4_ref, win_ref, gate_ref, qmla_ref, row_ref, kmla_ref, mq_ref = outs
    xb = x_ref[...].astype(BF16)
    cos, s_lo, s_hi = cos_ref[...], slo_ref[...], shi_ref[...]

    def seg(start, width):
        return _dot(xb, wp_ref[:, start:start + width])

    hq = (seg(P_NQ, 512) * (NSA_DK ** -0.5)).astype(BF16)
    if prompt:
        for h in range(NSA_HEADS):
            qn_ref[h] = hq[:, h * NSA_DK:(h + 1) * NSA_DK]
    else:
        qn_ref[...] = hq
    nkv = seg(P_NKV, 768)
    nkv4_ref[...] = nkv[:, :512]
    win_ref[...] = nkv[:, 512:768]
    if prompt:
        tm = nkv.shape[0]
        nkvb = nkv.astype(BF16)
        for g in range(NSA_GROUPS):
            kslc_ref[g] = nkvb[:, 256 + g * NSA_DK:256 + (g + 1) * NSA_DK]
            kwin_ref[g] = nkvb[:, 512 + g * NSA_DK:512 + (g + 1) * NSA_DK]
        vt = nkv[:, 384:512].T.astype(BF16)
        for c in range(tm // 256):
            vslct_ref[c] = vt[:, c * 256:(c + 1) * 256]
        wt = nkv[:, 640:768].T.astype(BF16)
        for c in range(tm // 128):
            vwint_ref[c] = wt[:, c * 128:(c + 1) * 128]
    gate_ref[...] = jax.nn.sigmoid(seg(P_NG, 128))
    cq = _rms(seg(P_QD, 256), gq_ref[...]).astype(BF16)
    q_nope = _dot(cq, wuq_ref[...]).astype(BF16)
    q_lat = _dot(q_nope, wuk_ref[...])
    q_rope = _dot(cq, wqr_ref[...])
    for h in range(MLA_HEADS):
        sl = slice(h * LANES, (h + 1) * LANES)
        qmla_ref[:, 2 * h * LANES:(2 * h + 1) * LANES] = q_lat[:, sl].astype(BF16)
        qmla_ref[:, (2 * h + 1) * LANES:(2 * h + 2) * LANES] = _rope(q_rope[:, sl], cos, s_lo, s_hi).astype(BF16)
    ckv = _rms(seg(P_KVD, 128), gkv_ref[...])
    kr = _rope(seg(P_KR, 128), cos, s_lo, s_hi)
    row_ref[:, :KV_LORA] = ckv
    row_ref[:, KV_LORA:] = kr[:, :D_ROPE]
    kmla_ref[:, :KV_LORA] = ckv.astype(BF16)
    kmla_ref[:, KV_LORA:] = kr.astype(BF16)
    mq_ref[...] = seg(P_MQ, 512).astype(BF16)


def _proj(x, wts, tables, tm, prompt):
    rows = x.shape[0]
    wp, wuq, wuk, wqr, gq, gkv = wts
    cos, s_lo, s_hi = tables
    row_spec = lambda w: pl.BlockSpec((tm, w), lambda i: (i, 0))
    out_shapes = [
        jax.ShapeDtypeStruct((NSA_HEADS, rows, NSA_DK) if prompt else (rows, 512), BF16),
        jax.ShapeDtypeStruct((rows, 512), F32),
        jax.ShapeDtypeStruct((rows, 256), F32),
        jax.ShapeDtypeStruct((rows, LANES), F32),
        jax.ShapeDtypeStruct((rows, 2048), BF16),
        jax.ShapeDtypeStruct((rows, MLA_ROW), F32),
        jax.ShapeDtypeStruct((rows, 256), BF16),
        jax.ShapeDtypeStruct((rows, 512), BF16),
    ]
    out_specs = [
        pl.BlockSpec((NSA_HEADS, tm, NSA_DK), lambda i: (0, i, 0)) if prompt else row_spec(512),
        row_spec(512), row_spec(256), row_spec(LANES), row_spec(2048), row_spec(MLA_ROW),
        row_spec(256), row_spec(512),
    ]
    if prompt:
        out_shapes += [
            jax.ShapeDtypeStruct((NSA_GROUPS, rows, NSA_DK), BF16),
            jax.ShapeDtypeStruct((rows // 256, 128, 256), BF16),
            jax.ShapeDtypeStruct((NSA_GROUPS, rows, NSA_DK), BF16),
            jax.ShapeDtypeStruct((rows // 128, 128, 128), BF16),
        ]
        out_specs += [
            pl.BlockSpec((NSA_GROUPS, tm, NSA_DK), lambda i: (0, i, 0)),
            pl.BlockSpec((tm // 256, 128, 256), lambda i: (i, 0, 0)),
            pl.BlockSpec((NSA_GROUPS, tm, NSA_DK), lambda i: (0, i, 0)),
            pl.BlockSpec((tm // 128, 128, 128), lambda i: (i, 0, 0)),
        ]
    return pl.pallas_call(
        functools.partial(_proj_body, prompt),
        grid=(rows // tm,),
        in_specs=[row_spec(D_MODEL), _full(wp.shape), _full(wuq.shape), _full(wuk.shape),
                  _full(wqr.shape), _full(gq.shape), _full(gkv.shape),
                  row_spec(LANES), row_spec(LANES), row_spec(LANES)],
        out_specs=out_specs,
        out_shape=out_shapes,
        compiler_params=_cparams(1),
        name="proj",
    )(x, wp, wuq, wuk, wqr, gq, gkv, cos, s_lo, s_hi)


def _rope_tables(pos):
    half = D_ROPE // 2
    freq = ROPE_THETA ** (-jnp.arange(half, dtype=F32) / half)
    ang = pos.astype(F32)[:, None] * freq
    cos, sin = jnp.cos(ang), jnp.sin(ang)
    z = jnp.zeros((pos.shape[0], LANES - D_ROPE), F32)
    zh = jnp.zeros_like(sin)
    return (jnp.concatenate([cos, cos, z], 1), jnp.concatenate([-sin, zh, z], 1),
            jnp.concatenate([zh, sin, z], 1))


def _proj_weights(w_in, g_q, w_uq, w_qr, g_kv, w_uk):
    offs = np.cumsum((0,) + IN_WIDTHS)
    col = lambda i: w_in[:, offs[i]:offs[i + 1]]
    pad = lambda a, w: jnp.pad(a, ((0, 0), (0, w - a.shape[1])))
    ng = col(2).reshape(D_MODEL, NSA_HEADS, 3).transpose(0, 2, 1).reshape(D_MODEL, 24)
    wp = jnp.concatenate([col(0), col(1), pad(ng, 128), col(3), col(4), pad(col(5), 128), col(6)], 1)
    wuq = w_uq.reshape(Q_LORA, MLA_HEADS * D_NOPE)
    eye = jnp.eye(MLA_HEADS, dtype=F32)
    wuk = jnp.einsum('rhd,hg->hdgr', w_uk, eye).reshape(MLA_HEADS * D_NOPE, MLA_HEADS * KV_LORA)
    wqr = jnp.pad(w_qr, ((0, 0), (0, 0), (0, LANES - D_ROPE))).reshape(Q_LORA, MLA_HEADS * LANES)
    return (wp.astype(BF16), wuq.astype(BF16), wuk.astype(BF16), wqr.astype(BF16),
            g_q.reshape(1, Q_LORA), g_kv.reshape(1, KV_LORA))


def _compress_weights(cmp_pos, cmp_w1, cmp_w2):
    eye = jnp.eye(NSA_GROUPS, dtype=F32)
    w1r = cmp_w1.reshape(2, CMP_BLOCK, NSA_DK, CMP_HIDDEN)
    w1 = jnp.einsum('ktdc,gG->ktgdGc', w1r, eye).reshape(2, CMP_BLOCK, 128, 256)
    w2 = jnp.einsum('kcd,gG->kgcGd', cmp_w2, eye).reshape(2, 256, 128)
    pos = jnp.concatenate([cmp_pos, cmp_pos], axis=-1)
    return pos, w1.astype(BF16), w2.astype(BF16)


def _compress_rows(x_ref, kv, nblk, pos_ref, w1_ref, w2_ref, transposed=False):
    acc = jnp.zeros((nblk, 256), F32)
    for t in range(CMP_BLOCK):
        xt = x_ref[pl.ds(t, nblk, stride=CMP_BLOCK), :]
        acc = acc + _dot((xt + pos_ref[kv, t:t + 1, :]).astype(BF16), w1_ref[kv, t])
    h = (acc * jax.nn.sigmoid(acc)).astype(BF16)
    if transposed:
        return _dot_nt(w2_ref[kv], h)
    return _dot(h, w2_ref[kv])


def _compress_prompt_body(xk_ref, xv_ref, pos_ref, w1_ref, w2_ref, w2t_ref, kc_ref, vct_ref):
    nblk = xk_ref.shape[0] // CMP_BLOCK
    kcb = _compress_rows(xk_ref, 0, nblk, pos_ref, w1_ref, w2_ref).astype(BF16)
    vt = _compress_rows(xv_ref, 1, nblk, pos_ref, w1_ref, w2t_ref, transposed=True).astype(BF16)
    for g in range(NSA_GROUPS):
        kc_ref[g] = kcb[:, g * NSA_DK:(g + 1) * NSA_DK]
        vct_ref[g] = vt[g * NSA_DK:(g + 1) * NSA_DK, :]


def _compress_prompt(nkv4, cw):
    t = nkv4.shape[0]
    rows = min(t, 4096)
    nblk, nc = rows // CMP_BLOCK, t // CMP_BLOCK
    pos, w1, w2 = cw
    w2t = w2.transpose(0, 2, 1)
    return pl.pallas_call(
        _compress_prompt_body,
        grid=(t // rows,),
        in_specs=[pl.BlockSpec((rows, 128), lambda i: (i, 0)), pl.BlockSpec((rows, 128), lambda i: (i, 1)),
                  _full(pos.shape), _full(w1.shape), _full(w2.shape), _full(w2t.shape)],
        out_specs=[pl.BlockSpec((NSA_GROUPS, nblk, NSA_DK), lambda i: (0, i, 0)),
                   pl.BlockSpec((NSA_GROUPS, NSA_DK, nblk), lambda i: (0, 0, i))],
        out_shape=[jax.ShapeDtypeStruct((NSA_GROUPS, nc, NSA_DK), BF16),
                   jax.ShapeDtypeStruct((NSA_GROUPS, NSA_DK, nc), BF16)],
        compiler_params=_cparams(1),
        name="compress_prompt",
    )(nkv4, nkv4, pos, w1, w2, w2t)


def _t5_bucket(dist):
    n = jnp.maximum(dist, 0)
    max_exact = N_BUCKETS // 2
    nf = jnp.maximum(n, 1).astype(F32)
    large = max_exact + (jnp.log(nf / max_exact) / math.log(MAX_DISTANCE / max_exact)
                         * (N_BUCKETS - max_exact)).astype(jnp.int32)
    large = jnp.minimum(large, N_BUCKETS - 1)
    return jnp.where(n < max_exact, n, large)


FAR_DIST = 129


def _rel_bias(rel_bias, dist):
    b = rel_bias[_t5_bucket(dist)] - rel_bias[N_BUCKETS - 1]
    b = jnp.where((dist >= 0)[..., None], b, NEG)
    return jnp.moveaxis(b, -1, 0)


def _lanes_hq(b):
    k = b.shape[1]
    return b.reshape(NSA_GROUPS, NSA_HPG, k, 128).transpose(0, 2, 1, 3).reshape(NSA_GROUPS, k, 512)


def _prompt_bias_tables(rel_bias):
    q = jnp.arange(128)[None, :]
    k = jnp.arange(128)[:, None]
    zero = jnp.zeros((NSA_GROUPS, 128, 512), F32)
    sub = _lanes_hq(_rel_bias(rel_bias, 128 + q - k))
    diag = _lanes_hq(_rel_bias(rel_bias, q - k))
    neg = jnp.full((NSA_GROUPS, 128, 512), NEG, F32)
    anti = _lanes_hq(jnp.broadcast_to(jnp.where(k >= q, 0.0, NEG)[None], (8, 128, 128)))
    near = jnp.stack([zero, sub, diag, neg, anti], axis=1)
    r = jnp.arange(16)[:, None]
    cmpw = jnp.stack([_lanes_hq(_rel_bias(rel_bias, q - CMP_BLOCK * (r - off) - (CMP_BLOCK - 1)))
                      for off in (4, 8, 0)], axis=1)
    return near, cmpw


T_ZERO, T_SUB, T_DIAG, T_NEG, T_ANTI = range(5)


def _softmax_update(s, vt, m_old, l_old, acc_old):
    m_new = jnp.maximum(m_old, jnp.max(s, axis=0, keepdims=True))
    alpha = jnp.exp(m_old - m_new)
    e = jnp.exp(s - m_new)
    return (m_new, alpha * l_old + jnp.sum(e, axis=0, keepdims=True),
            alpha * acc_old + _dot(vt, e.astype(BF16)))


def _topk_rows(score, n_sel):
    nb = score.shape[0]
    blk = lax.broadcasted_iota(jnp.int32, score.shape, 0)
    sel = jnp.zeros(score.shape, F32)
    picks = []
    for _ in range(n_sel):
        m = jnp.max(score, axis=0, keepdims=True)
        j = jnp.min(jnp.where(score == m, blk, nb), axis=0, keepdims=True)
        hit = blk == j
        sel = jnp.where(hit, 1.0, sel)
        score = jnp.where(hit, -2.0, score)
        picks.append(j)
    return sel, picks


def _nsa_prompt_body(q_ref, gate_ref, kc_ref, vct_ref, kslc_ref, vslct_ref, kwin_ref, vwint_ref,
                     near_ref, cmpw_ref, o_ref, s_ref, imp_ref, sel_ref, m_ref, l_ref, acc_ref, out_ref):
    i = pl.program_id(0)
    nc = kc_ref.shape[1]
    nb = nc // 2
    gate_t = gate_ref[...].T
    lane_q = lax.broadcasted_iota(jnp.int32, (1, 128), 1)
    cur = 2 * i + (lane_q >= SEL_BLOCK).astype(jnp.int32)
    odd = i % 2
    w0 = pl.multiple_of(jnp.where(i == 0, 0, jnp.where(odd == 1, 4 * i - 4, 4 * i - 8)), 8)
    var = jnp.where(i == 0, 2, jnp.where(odd == 1, 0, 1))
    jl = i // 2
    even = 1 - odd

    def reset():
        m_ref[...] = jnp.full(m_ref.shape, M_INIT, F32)
        l_ref[...] = jnp.zeros(l_ref.shape, F32)
        acc_ref[...] = jnp.zeros(acc_ref.shape, F32)

    def gate_row(g, b):
        return jnp.concatenate([gate_t[b * 8 + 4 * g + h:b * 8 + 4 * g + h + 1, :]
                                for h in range(NSA_HPG)], axis=1)

    groups = range(NSA_GROUPS)

    def q_group(g):
        return q_ref[4 * g:4 * g + 4].reshape(4 * 128, NSA_DK)

    for g in groups:
        s_ref[g, pl.ds(nc, 16), :] = jnp.zeros((16, 512), F32)
        s_ref[g, pl.ds(0, nc), :] = _dot_nt(kc_ref[g], q_group(g))
        s_ref[g, pl.ds(w0, 16), :] = s_ref[g, pl.ds(w0, 16), :] + cmpw_ref[g, var]
        row_n = lax.broadcasted_iota(jnp.int32, (nc, 1), 0)
        s = jnp.where(row_n <= 4 * i + 3, s_ref[g, pl.ds(0, nc), :], NEG)
        m = jnp.maximum(jnp.max(s, axis=0, keepdims=True), M_INIT)
        e = jnp.exp(s - m)
        d = jnp.sum(e, axis=0, keepdims=True)
        p = e / jnp.where(d > 0.0, d, 1.0)
        out_ref[g] = gate_row(g, 0) * _dot(vct_ref[g], p.astype(BF16))
        imp_ref[g] = p[:, 0:128] + p[:, 128:256] + p[:, 256:384] + p[:, 384:512]
        imp = imp_ref[g, pl.ds(0, nb, stride=2), :] + imp_ref[g, pl.ds(1, nb, stride=2), :]
        blk = lax.broadcasted_iota(jnp.int32, (nb, 128), 0)
        valid = blk <= cur
        forced = valid & ((blk == 0) | (blk == cur) | (blk == cur - 1))
        score = jnp.where(forced, NSA_HPG + 1.0, jnp.where(valid, imp, -1.0))
        sel_ref[g], _ = _topk_rows(score, min(N_SEL, nb))

    def sel_qk(g, j):
        return _dot_nt(kslc_ref[g, pl.ds(pl.multiple_of(j * 256, 256), 256), :], q_group(g))

    def sel_scores(g, j, top, bot, s=None):
        if s is None:
            s = sel_qk(g, j)
        parts = []
        for b in range(4):
            row = sel_ref[g, pl.ds(4 * j + b, 1), :]
            mask = jnp.concatenate([row] * NSA_HPG, axis=1) > 0.5
            sb = s[b * SEL_BLOCK:(b + 1) * SEL_BLOCK]
            if top is not None:
                tab = near_ref[g, top if b < 2 else bot]
                sb = sb + tab[(b % 2) * SEL_BLOCK:(b % 2 + 1) * SEL_BLOCK]
            parts.append(jnp.where(mask, sb, NEG))
        return jnp.concatenate(parts, axis=0), vslct_ref[j, g * NSA_DK:(g + 1) * NSA_DK, :]

    def pipelined(items, qk, finish, carry):
        carry = list(carry)
        raw = [qk(items[0])]
        for n, item in enumerate(items):
            if n + 1 < len(items):
                raw.append(qk(items[n + 1]))
            g = item[0]
            carry[g] = _softmax_update(*finish(item, raw[n]), *carry[g])
        return tuple(carry)

    def load_state():
        return tuple((m_ref[g], l_ref[g], acc_ref[g]) for g in groups)

    def store_state(st):
        for g in groups:
            m_ref[g], l_ref[g], acc_ref[g] = st[g]

    def sel_items(chunks):
        return [(g, j, top, bot) for (j, top, bot) in chunks for g in groups]

    sel_run = functools.partial(pipelined, qk=lambda it: sel_qk(it[0], it[1]),
                                finish=lambda it, raw: sel_scores(*it, raw))
    n_far = jnp.maximum(jl - 1, 0)
    unroll = 4
    init = tuple((jnp.full((1, 512), M_INIT, F32), jnp.zeros((1, 512), F32),
                  jnp.zeros((NSA_DK, 512), F32)) for _ in groups)
    far = lax.fori_loop(
        0, n_far // unroll,
        lambda jj, c: sel_run(sel_items([(unroll * jj + u, None, None) for u in range(unroll)]), carry=c), init)
    far = lax.fori_loop(n_far - n_far % unroll, n_far,
                        lambda j, c: sel_run(sel_items([(j, None, None)]), carry=c), far)
    store_state(far)
    last_tabs = (jl, jnp.where(even == 1, T_DIAG, T_SUB), jnp.where(even == 1, T_NEG, T_DIAG))

    @pl.when(jl >= 1)
    def _():
        prev_tabs = (jl - 1, T_ZERO, jnp.where(even == 1, T_SUB, T_ZERO))
        store_state(sel_run(sel_items([prev_tabs, last_tabs]), carry=load_state()))

    @pl.when(jl < 1)
    def _():
        store_state(sel_run(sel_items([last_tabs]), carry=load_state()))

    for g in groups:
        out_ref[g] = out_ref[g] + (gate_row(g, 1) / l_ref[g]) * acc_ref[g]

    def win_qk(it):
        g, j, _ = it
        return _dot_nt(kwin_ref[g, pl.ds(pl.multiple_of(j * 128, 128), 128), :], q_group(g))

    def win_finish(it, s):
        g, j, tab = it
        if tab is not None:
            s = s + near_ref[g, tab]
        return s, vwint_ref[j, g * NSA_DK:(g + 1) * NSA_DK, :]

    win_chunks = ((4, T_ANTI), (3, None), (2, None), (1, T_SUB), (0, T_DIAG))
    reset()

    @pl.when(i >= 4)
    def _():
        items = [(g, i - back, tab) for back, tab in win_chunks for g in groups]
        store_state(pipelined(items, win_qk, win_finish, load_state()))

    @pl.when(i < 4)
    def _():
        for back, tab in win_chunks:
            @pl.when(i >= back)
            def _(back=back, tab=tab):
                items = [(g, i - back, tab) for g in groups]
                store_state(pipelined(items, win_qk, win_finish, load_state()))

    for g in groups:
        out = out_ref[g] + (gate_row(g, 2) / l_ref[g]) * acc_ref[g]
        for h in range(NSA_HPG):
            col = (4 * g + h) * NSA_DK
            o_ref[:, col:col + NSA_DK] = out[:, h * 128:(h + 1) * 128].T.astype(BF16)


def _nsa_prompt(qn, gate, kc, vct, kslc, vslct, kwin, vwint, near, cmpw):
    t = qn.shape[1]
    nc = kc.shape[1]
    return pl.pallas_call(
        _nsa_prompt_body,
        grid=(t // 128,),
        in_specs=[pl.BlockSpec((NSA_HEADS, 128, NSA_DK), lambda i: (0, i, 0)),
                  pl.BlockSpec((128, LANES), lambda i: (i, 0)),
                  _full(kc.shape), _full(vct.shape), _full(kslc.shape), _full(vslct.shape),
                  _full(kwin.shape), _full(vwint.shape), _full(near.shape), _full(cmpw.shape)],
        out_specs=pl.BlockSpec((128, 512), lambda i: (i, 0)),
        out_shape=jax.ShapeDtypeStruct((t, 512), BF16),
        scratch_shapes=[pltpu.VMEM((NSA_GROUPS, nc + 16, 512), F32), pltpu.VMEM((NSA_GROUPS, nc, 128), F32),
                        pltpu.VMEM((NSA_GROUPS, nc // 2, 128), F32), pltpu.VMEM((NSA_GROUPS, 1, 512), F32),
                        pltpu.VMEM((NSA_GROUPS, 1, 512), F32), pltpu.VMEM((NSA_GROUPS, NSA_DK, 512), F32),
                        pltpu.VMEM((NSA_GROUPS, NSA_DK, 512), F32)],
        compiler_params=_cparams(1),
        name="nsa_prompt",
    )(qn, gate, kc, vct, kslc, vslct, kwin, vwint, near, cmpw)


MLA_TQ = 128
MLA_KC = 512
MLA_SLAB_HEADS = 8
MLA_SCALE = (D_NOPE + D_ROPE) ** -0.5


def _mla_prompt_body(q_ref, k_ref, o_ref):
    i = pl.program_id(0)
    last = (i * MLA_TQ) // MLA_KC
    n_slab = MLA_HEADS // MLA_SLAB_HEADS
    rows = MLA_SLAB_HEADS * MLA_TQ

    def q_slab(sl):
        return jnp.concatenate([q_ref[:, (sl * MLA_SLAB_HEADS + h) * 256:(sl * MLA_SLAB_HEADS + h + 1) * 256]
                                for h in range(MLA_SLAB_HEADS)], axis=0)

    def keys(j):
        return k_ref[pl.ds(pl.multiple_of(j * MLA_KC, MLA_KC), MLA_KC), :]

    def scores(j):
        kc = keys(j)
        return tuple(_dot_nt(q_slab(sl), kc) for sl in range(n_slab))

    def update(j, raw, state, causal):
        kc = keys(j)
        new = []
        for sl in range(n_slab):
            m_old, l_old, acc = state[sl]
            s = raw[sl] * MLA_SCALE
            if causal:
                q_pos = i * MLA_TQ + lax.broadcasted_iota(jnp.int32, (rows, 1), 0) % MLA_TQ
                k_pos = j * MLA_KC + lax.broadcasted_iota(jnp.int32, (1, MLA_KC), 1)
                s = jnp.where(k_pos <= q_pos, s, NEG)
            m_new = jnp.maximum(m_old, jnp.max(s, axis=1, keepdims=True))
            alpha = jnp.exp(m_old - m_new)
            e = jnp.exp(s - m_new)
            l_new = alpha * l_old + jnp.sum(e, axis=1, keepdims=True)
            new.append((m_new, l_new, alpha * acc + _dot(e.astype(BF16), kc[:, :KV_LORA])))
        return tuple(new)

    init = tuple((jnp.full((rows, 1), M_INIT, F32), jnp.zeros((rows, 1), F32),
                  jnp.zeros((rows, KV_LORA), F32)) for _ in range(n_slab))
    state = lax.fori_loop(0, last, lambda j, st: update(j, scores(j), st, False), init)
    final = update(last, scores(last), state, True)
    for sl in range(n_slab):
        _, l, acc = final[sl]
        o = acc / l
        for h in range(MLA_SLAB_HEADS):
            hh = sl * MLA_SLAB_HEADS + h
            o_ref[:, hh * KV_LORA:(hh + 1) * KV_LORA] = o[h * MLA_TQ:(h + 1) * MLA_TQ].astype(BF16)


def _mla_prompt(qmla, kmla):
    t = qmla.shape[0]
    return pl.pallas_call(
        _mla_prompt_body,
        grid=(t // MLA_TQ,),
        in_specs=[pl.BlockSpec((MLA_TQ, 2048), lambda i: (i, 0)), _full(kmla.shape)],
        out_specs=pl.BlockSpec((MLA_TQ, MLA_HEADS * KV_LORA), lambda i: (i, 0)),
        out_shape=jax.ShapeDtypeStruct((t, MLA_HEADS * KV_LORA), BF16),
        compiler_params=_cparams(1),
        name="mla_prompt",
    )(qmla, kmla)


def _mem_kv_body(x_ref, w_ref, o_ref, ob_ref):
    kv = _dot(x_ref[...].astype(BF16), w_ref[...])
    o_ref[...] = kv
    ob_ref[...] = kv.astype(BF16)


def _mem_kv(mem, w):
    shp = (MEM_TOKENS, 2 * MEM_HEADS * MEM_DH)
    return pl.pallas_call(
        _mem_kv_body,
        in_specs=[_full(mem.shape), _full(w.shape)],
        out_specs=[_full(shp), _full(shp)],
        out_shape=[jax.ShapeDtypeStruct(shp, F32), jax.ShapeDtypeStruct(shp, BF16)],
        grid=(1,),
        compiler_params=_cparams(1),
        name="mem_kv",
    )(mem, w)


def _mem_attn_body(q_ref, kv_ref, o_ref):
    for h in range(MEM_HEADS):
        sl = slice(h * MEM_DH, (h + 1) * MEM_DH)
        k = kv_ref[:, sl]
        v = kv_ref[:, MEM_HEADS * MEM_DH + h * MEM_DH:MEM_HEADS * MEM_DH + (h + 1) * MEM_DH]
        s = _dot_nt(q_ref[:, sl], k) * (MEM_DH ** -0.5)
        e = jnp.exp(s - jnp.max(s, axis=1, keepdims=True))
        p = e / jnp.sum(e, axis=1, keepdims=True)
        o_ref[:, sl] = _dot(p.astype(BF16), v).astype(BF16)


def _mem_attn(mq, kvb, tm):
    t = mq.shape[0]
    return pl.pallas_call(
        _mem_attn_body,
        grid=(t // tm,),
        in_specs=[pl.BlockSpec((tm, 512), lambda i: (i, 0)), _full(kvb.shape)],
        out_specs=pl.BlockSpec((tm, 512), lambda i: (i, 0)),
        out_shape=jax.ShapeDtypeStruct((t, 512), BF16),
        compiler_params=_cparams(1),
        name="mem_attn",
    )(mq, kvb)


def _merge_weights(w_in, w_uv, w_br, w_o):
    wmg = w_in[:, sum(IN_WIDTHS[:-1]):]
    eye = jnp.eye(MLA_HEADS, dtype=F32)
    wuv = jnp.einsum('rhd,hg->hrgd', w_uv, eye).reshape(MLA_HEADS * KV_LORA, MLA_HEADS * D_V)
    return wmg.astype(BF16), wuv.astype(BF16), w_br.astype(BF16), w_o.astype(BF16)


def _merge_body(x_ref, onsa_ref, olat_ref, omem_ref, wmg_ref, wuv_ref, wbr_ref, wo_ref, g_ref, b_ref,
                o_ref):
    x = x_ref[...]
    xb = x.astype(BF16)
    v_mla = _dot(olat_ref[...], wuv_ref[...]).astype(BF16)
    tot = jnp.zeros(x.shape, F32)
    for b, br in enumerate((onsa_ref[...], v_mla, omem_ref[...])):
        gate = jax.nn.sigmoid(_dot(xb, wmg_ref[:, b * D_MODEL:(b + 1) * D_MODEL]))
        tot = tot + gate * _dot(br, wbr_ref[b])
    mix = _dot(tot.astype(BF16), wo_ref[...])
    o_ref[...] = _layer_norm(ALPHA * x + mix, g_ref[...], b_ref[...])


def _merge(x, onsa, olat, omem, mw, g, b, tm):
    rows = x.shape[0]
    wmg, wuv, wbr, wo = mw
    row_spec = lambda w: pl.BlockSpec((tm, w), lambda i: (i, 0))
    return pl.pallas_call(
        _merge_body,
        grid=(rows // tm,),
        in_specs=[row_spec(D_MODEL), row_spec(512), row_spec(1024), row_spec(512),
                  _full(wmg.shape), _full(wuv.shape), _full(wbr.shape), _full(wo.shape),
                  _full(g.shape), _full(b.shape)],
        out_specs=row_spec(D_MODEL),
        out_shape=jax.ShapeDtypeStruct((rows, D_MODEL), F32),
        compiler_params=_cparams(1),
        name="merge",
    )(x, onsa, olat, omem, mw[0], wuv, wbr, wo, g, b)


def _prompt_path(x_prompt, mem_prompt, rel_bias, ln_g, ln_b, ffn_w, pw, cw, mw, mem_w_kv):
    t = x_prompt.shape[1]
    tm = 512
    ln = lambda k: (ln_g[0, k][None], ln_b[0, k][None])
    x1 = _ffn(x_prompt[0], *ffn_w[0], *ln(0), tm)
    (qn, nkv4, win, gate, qmla, row, kmla, mq, kslc, vslct, kwin, vwint) = _proj(
        x1, pw, _rope_tables(jnp.arange(t)), tm, True)
    kc, vct = _compress_prompt(nkv4, cw)
    near, cmpw = _prompt_bias_tables(rel_bias)
    o_nsa = _nsa_prompt(qn, gate, kc, vct, kslc, vslct, kwin, vwint, near, cmpw)
    o_lat = _mla_prompt(qmla, kmla)
    mem_kv, mem_kvb = _mem_kv(mem_prompt[0], mem_w_kv[0].reshape(D_MODEL, -1).astype(BF16))
    o_mem = _mem_attn(mq, mem_kvb, tm)
    x2 = _merge(x1, o_nsa, o_lat, o_mem, mw, *ln(1), tm)
    y = _ffn(x2, *ffn_w[1], *ln(2), tm)
    wb = min(WINDOW, t)
    return (y[None],
            nkv4.reshape(1, 1, t, 4, NSA_GROUPS, NSA_DK),
            row.reshape(1, 1, t, MLA_ROW),
            win[t - wb:].reshape(1, 1, wb, 2, NSA_GROUPS, NSA_DK),
            mem_kv.reshape(1, 1, MEM_TOKENS, 2, MEM_HEADS, MEM_DH))


def _own_group_lanes(q):
    q2 = jnp.concatenate([q, q], axis=1)
    head = lax.broadcasted_iota(jnp.int32, q2.shape, 0)
    lane = lax.broadcasted_iota(jnp.int32, q2.shape, 1)
    return jnp.where(head // NSA_HPG == lane // NSA_DK, q2, jnp.zeros_like(q2))


def _bf(x):
    return x.astype(BF16)


def _new_key_score(qm, k_new):
    return jnp.sum(qm.astype(F32) * _bf(k_new).astype(F32), axis=1, keepdims=True)


CMP_PITCH = 136


def _sample_compress_weights(cmp_pos, cmp_w1, cmp_w2):
    eye4 = jnp.eye(PAGE // CMP_BLOCK, dtype=F32)
    w1p = cmp_w1.reshape(2, CMP_BLOCK, NSA_DK // 2, 2, CMP_HIDDEN)
    w1 = jnp.einsum('ktpjc,nm->kpjntmc', w1p, eye4).reshape(2, NSA_DK // 2, 256, 512)
    eye2 = jnp.eye(NSA_GROUPS, dtype=F32)
    w2 = jnp.einsum('kcd,gG->kgcGd', cmp_w2, eye2).reshape(2, 256, 128)
    pos = jnp.tile(cmp_pos.transpose(0, 2, 1), (1, 1, PAGE // CMP_BLOCK))
    return pos, w1.astype(BF16), w2.astype(BF16)


def _s_compress_body(pt_ref, cache_ref, pos_ref, w1_ref, w2_ref, out_ref, buf, sem):
    kv, b = pl.program_id(0), pl.program_id(1)
    db = pl.num_programs(1)
    n_pages = pt_ref.shape[1]
    n = kv * db + b
    slot = n % 2

    def copies(step, sl, wait):
        skv, sb = step // db, step % db

        def one(p, c):
            cp = pltpu.make_async_copy(
                cache_ref.at[pt_ref[sb, p], pl.ds(pl.multiple_of(skv * 128, 128), 128), :],
                buf.at[sl, pl.ds(pl.multiple_of(p * CMP_PITCH, 8), 128), :], sem.at[sl])
            cp.wait() if wait else cp.start()
            return c

        lax.fori_loop(0, n_pages, one, 0)

    @pl.when(n == 0)
    def _():
        copies(0, 0, False)

    @pl.when(n + 1 < 2 * db)
    def _():
        copies(n + 1, 1 - slot, False)

    copies(n, slot, True)
    page_rows = buf.at[slot]
    acc = jnp.zeros((NSA_GROUPS * n_pages, 512), F32)
    for dp in range(NSA_DK // 2):
        rows = []
        for g in range(NSA_GROUPS):
            pair = [page_rows[pl.ds(g * NSA_DK + 2 * dp + j, n_pages, stride=CMP_PITCH), :]
                    + pos_ref[0, 2 * dp + j:2 * dp + j + 1, :] for j in range(2)]
            rows.append(jnp.concatenate(pair, axis=1))
        acc = acc + _dot(_bf(jnp.concatenate(rows, axis=0)), w1_ref[0, dp])
    h = _bf(acc * jax.nn.sigmoid(acc))
    for nb in range(PAGE // CMP_BLOCK):
        hh = jnp.concatenate([h[g * n_pages:(g + 1) * n_pages, nb * 128:(nb + 1) * 128]
                              for g in range(NSA_GROUPS)], axis=1)
        out_ref[0, 0, nb] = _bf(_dot(hh, w2_ref[0]))


def _s_compress(page_table, cache_t, cw):
    db, n_pages = page_table.shape
    pos, w1, w2 = cw
    kvspec = lambda shape: pl.BlockSpec((1,) + shape[1:], lambda kv, b, pt: (kv,) + (0,) * (len(shape) - 1))
    nbk = PAGE // CMP_BLOCK
    return pl.pallas_call(
        _s_compress_body,
        grid_spec=pltpu.PrefetchScalarGridSpec(
            num_scalar_prefetch=1,
            grid=(2, db),
            in_specs=[pl.BlockSpec(memory_space=pl.ANY), kvspec(pos.shape), kvspec(w1.shape), kvspec(w2.shape)],
            out_specs=pl.BlockSpec((1, 1, nbk, n_pages, 128), lambda kv, b, pt: (kv, b, 0, 0, 0)),
            scratch_shapes=[pltpu.VMEM((2, n_pages * CMP_PITCH, 128), F32), pltpu.SemaphoreType.DMA((2,))],
        ),
        out_shape=jax.ShapeDtypeStruct((2, db, nbk, n_pages, 128), BF16),
        compiler_params=_cparams(2),
        name="sample_compress",
    )(page_table, cache_t, pos, w1, w2)


def _s_cmp_win_body(cmp_ref, q_ref, g_ref, st_ref, wnew_ref, bc_ref, bw_ref, bn_ref, o8_ref, imp_ref):
    nc = bc_ref.shape[1]
    kc = cmp_ref[0, 0].reshape(nc, 128)
    vc = cmp_ref[1, 0].reshape(nc, 128)
    qm = _own_group_lanes(q_ref[0])
    gate = g_ref[0]
    s = _dot_nt(qm, kc) + bc_ref[...]
    e = jnp.exp(s - jnp.max(s, axis=1, keepdims=True))
    p = e / jnp.sum(e, axis=1, keepdims=True)
    out = gate[:, 0:1] * _dot(_bf(p), vc)
    for g in range(NSA_GROUPS):
        imp_ref[0, g:g + 1, :] = jnp.sum(p[NSA_HPG * g:NSA_HPG * (g + 1)], axis=0, keepdims=True)
    st = st_ref[0]
    wnew = wnew_ref[0]
    s = _dot(qm, _bf(st[:128])) + bw_ref[...]
    s_new = _new_key_score(qm, wnew[:, :128]) + bn_ref[...]
    m = jnp.maximum(jnp.max(s, axis=1, keepdims=True), s_new)
    e = jnp.exp(s - m)
    e_new = jnp.exp(s_new - m)
    l = jnp.sum(e, axis=1, keepdims=True) + e_new
    acc = _dot_nt(_bf(e), _bf(st[128:])) + _bf(e_new).astype(F32) * _bf(wnew[:, 128:]).astype(F32)
    o8_ref[0] = out + (gate[:, 2:3] / l) * acc


def _s_cmp_win(cmp, q3, g3, state_t, wnew, bc, bw, bn):
    db = q3.shape[0]
    nc = bc.shape[1]
    wb = state_t.shape[2]
    one = lambda *tail: pl.BlockSpec((1,) + tail, lambda b: (b,) + (0,) * len(tail))
    return pl.pallas_call(
        _s_cmp_win_body,
        grid=(db,),
        in_specs=[pl.BlockSpec((2, 1) + cmp.shape[2:], lambda b: (0, b, 0, 0, 0)),
                  one(NSA_HEADS, NSA_DK), one(NSA_HEADS, 3), one(256, wb), one(1, 256),
                  _full(bc.shape), _full(bw.shape), _full(bn.shape)],
        out_specs=[one(NSA_HEADS, 128), one(NSA_GROUPS, nc)],
        out_shape=[jax.ShapeDtypeStruct((db, NSA_HEADS, 128), F32),
                   jax.ShapeDtypeStruct((db, NSA_GROUPS, nc), F32)],
        compiler_params=_cparams(1),
        name="sample_cmp_win",
    )(cmp, q3, g3, state_t, wnew, bc, bw, bn)


def _s_topk_body(n_blk, imp_ref, idx_ref, t_ref):
    db, nc = imp_ref.shape[0], imp_ref.shape[2]
    nbp = nc // 2
    rows = t_ref.shape[0] // 2
    cur = n_blk - 1
    for g in range(NSA_GROUPS):
        t_ref[pl.ds(0, nc), :] = imp_ref[:, g, :].T
        t_ref[pl.ds(nc, t_ref.shape[0] - nc), :] = jnp.zeros((t_ref.shape[0] - nc, db), F32)
        imp = t_ref[pl.ds(0, rows, stride=2), :] + t_ref[pl.ds(1, rows, stride=2), :]
        blk = lax.broadcasted_iota(jnp.int32, (rows, db), 0)
        forced = (blk == 0) | (blk == cur) | (blk == cur - 1)
        score = jnp.where(blk >= n_blk, -2.0, jnp.where(forced, NSA_HPG + 1.0, imp))
        _, picks = _topk_rows(score, min(N_SEL, n_blk))
        idx_ref[g] = jnp.concatenate(picks, axis=0)


def _s_topk(imp, n_blk):
    db, _, nc = imp.shape
    rows = -(-n_blk // 8) * 8
    n_sel = min(N_SEL, n_blk)
    return pl.pallas_call(
        functools.partial(_s_topk_body, n_blk),
        grid=(1,),
        in_specs=[_full(imp.shape)],
        out_specs=_full((NSA_GROUPS, n_sel, db)),
        out_shape=jax.ShapeDtypeStruct((NSA_GROUPS, n_sel, db), jnp.int32),
        scratch_shapes=[pltpu.VMEM((2 * rows, db), F32)],
        compiler_params=_cparams(1),
        name="sample_topk",
    )(imp)


def _s_select_body(n_sel, pt_ref, idx_ref, cache_ref, q_ref, g_ref, new_ref, o8in_ref, ta_ref, tb_ref,
                   bn_ref, o8_ref, buf, sem):
    b = pl.program_id(0)
    nbp = pt_ref.shape[1] * (PAGE // SEL_BLOCK)
    slot = b % 2
    n_dma = NSA_GROUPS * n_sel

    def copy(seq, sl, n):
        blk = jnp.minimum(idx_ref[n, seq], nbp - 1)
        return pltpu.make_async_copy(cache_ref.at[pt_ref[seq, blk // 2], pl.ds(256, 256), :],
                                     buf.at[sl, n // n_sel, :, pl.ds((n % n_sel) * PAGE, PAGE)], sem.at[sl])

    def start(seq, sl):
        for n in range(n_dma):
            copy(seq, sl, n).start()

    @pl.when(b == 0)
    def _():
        start(0, 0)

    @pl.when(b + 1 < pl.num_programs(0))
    def _():
        start(b + 1, 1 - slot)

    for n in range(n_dma):
        copy(b, slot, n).wait()
    qm = _own_group_lanes(q_ref[0])
    gate = g_ref[0]
    new = new_ref[0]
    head = lax.broadcasted_iota(jnp.int32, (NSA_HEADS, 1), 0)
    upper = (lax.broadcasted_iota(jnp.int32, (1, PAGE), 1) >= SEL_BLOCK).astype(F32)
    s_new = _new_key_score(qm, new[:, 256:384]) + bn_ref[...]
    v_new = _bf(new[:, 384:512]).astype(F32)
    out = o8in_ref[0]
    for g in range(NSA_GROUPS):
        pieces = []
        for r in range(n_sel):
            blk = idx_ref[g * n_sel + r, b]
            wa = jnp.where(blk == nbp - 1, 1.0, 0.0)
            wb = jnp.where(blk == nbp - 2, 1.0, 0.0)
            wm = jnp.where(blk >= nbp, NEG, 0.0)
            hb = (jnp.minimum(blk, nbp - 1) % 2).astype(F32)
            other_half = upper + hb - 2.0 * upper * hb
            pieces.append(wa * ta_ref[...] + wb * tb_ref[...] + wm + other_half * NEG)
        s = _dot(qm, _bf(buf[slot, g, pl.ds(0, 128), :])) + jnp.concatenate(pieces, axis=1)
        m = jnp.maximum(jnp.max(s, axis=1, keepdims=True), s_new)
        e = jnp.exp(s - m)
        e_new = jnp.exp(s_new - m)
        l = jnp.sum(e, axis=1, keepdims=True) + e_new
        acc = _dot_nt(_bf(e), _bf(buf[slot, g, pl.ds(128, 128), :])) + _bf(e_new).astype(F32) * v_new
        out = out + jnp.where(head // NSA_HPG == g, (gate[:, 1:2] / l) * acc, 0.0)
    o8_ref[0] = out


def _s_select(page_table, idx2, cache, q3, g3, new4, o8, ta, tb, bn):
    db = page_table.shape[0]
    n_sel = idx2.shape[0] // NSA_GROUPS
    one = lambda *tail: pl.BlockSpec((1,) + tail, lambda b, pt, ix: (b,) + (0,) * len(tail))
    fullp = lambda shape: pl.BlockSpec(shape, lambda b, pt, ix: (0,) * len(shape))
    return pl.pallas_call(
        functools.partial(_s_select_body, n_sel),
        grid_spec=pltpu.PrefetchScalarGridSpec(
            num_scalar_prefetch=2,
            grid=(db,),
            in_specs=[pl.BlockSpec(memory_space=pl.ANY), one(NSA_HEADS, NSA_DK), one(NSA_HEADS, 3),
                      one(1, 512), one(NSA_HEADS, 128), fullp(ta.shape), fullp(tb.shape), fullp(bn.shape)],
            out_specs=one(NSA_HEADS, 128),
            scratch_shapes=[pltpu.VMEM((2, NSA_GROUPS, 256, n_sel * PAGE), F32),
                            pltpu.SemaphoreType.DMA((2,))],
        ),
        out_shape=jax.ShapeDtypeStruct((db, NSA_HEADS, 128), F32),
        compiler_params=_cparams(1),
        name="sample_select",
    )(page_table, idx2, cache, q3, g3, new4, o8, ta, tb, bn)


MLA_S_PAGES = 32


def _s_mla_body(pt_ref, cache_ref, q_ref, knew_ref, o_ref, buf, sem, m_ref, l_ref, acc_ref):
    b, c = pl.program_id(0), pl.program_id(1)
    ncb = pl.num_programs(1)
    n_pages = buf.shape[2] // PAGE
    n = b * ncb + c
    slot = n % 2

    def copy(step, sl, p):
        return pltpu.make_async_copy(cache_ref.at[pt_ref[step // ncb, (step % ncb) * n_pages + p]],
                                     buf.at[sl, :, pl.ds(p * PAGE, PAGE)], sem.at[sl])

    def start(step, sl):
        for p in range(n_pages):
            copy(step, sl, p).start()

    @pl.when(n == 0)
    def _():
        start(0, 0)

    @pl.when(n + 1 < pl.num_programs(0) * ncb)
    def _():
        start(n + 1, 1 - slot)

    @pl.when(c == 0)
    def _():
        m_ref[...] = jnp.full(m_ref.shape, M_INIT, F32)
        l_ref[...] = jnp.zeros(l_ref.shape, F32)
        acc_ref[...] = jnp.zeros(acc_ref.shape, F32)

    for p in range(n_pages):
        copy(n, slot, p).wait()
    q = q_ref[0][:, :MLA_ROW]
    kt = _bf(buf[slot])
    s = _dot(q, kt) * MLA_SCALE
    m_old = m_ref[...]
    m_new = jnp.maximum(m_old, jnp.max(s, axis=1, keepdims=True))
    alpha = jnp.exp(m_old - m_new)
    e = jnp.exp(s - m_new)
    l_ref[...] = alpha * l_ref[...] + jnp.sum(e, axis=1, keepdims=True)
    acc_ref[...] = alpha * acc_ref[...] + _dot_nt(_bf(e), kt[:KV_LORA])
    m_ref[...] = m_new

    @pl.when(c == ncb - 1)
    def _():
        k_new = knew_ref[0]
        s_new = jnp.sum(q_ref[0].astype(F32) * k_new.astype(F32), axis=1, keepdims=True) * MLA_SCALE
        m_old = m_ref[...]
        m_new = jnp.maximum(m_old, s_new)
        alpha = jnp.exp(m_old - m_new)
        e_new = jnp.exp(s_new - m_new)
        l = alpha * l_ref[...] + e_new
        acc = alpha * acc_ref[...] + _bf(e_new).astype(F32) * k_new[:, :KV_LORA].astype(F32)
        o_ref[0] = _bf(acc / l)


def _s_mla(page_table, cache, q3, knew):
    db, n_pages = page_table.shape
    step_pages = min(MLA_S_PAGES, n_pages)
    ncb = n_pages // step_pages
    one = lambda *tail: pl.BlockSpec((1,) + tail, lambda b, c, pt: (b,) + (0,) * len(tail))
    return pl.pallas_call(
        _s_mla_body,
        grid_spec=pltpu.PrefetchScalarGridSpec(
            num_scalar_prefetch=1,
            grid=(db, ncb),
            in_specs=[pl.BlockSpec(memory_space=pl.ANY), one(MLA_HEADS, 256), one(1, 256)],
            out_specs=one(MLA_HEADS, KV_LORA),
            scratch_shapes=[pltpu.VMEM((2, MLA_ROW, step_pages * PAGE), F32),
                            pltpu.SemaphoreType.DMA((2,)), pltpu.VMEM((MLA_HEADS, 1), F32),
                            pltpu.VMEM((MLA_HEADS, 1), F32), pltpu.VMEM((MLA_HEADS, KV_LORA), F32)],
        ),
        out_shape=jax.ShapeDtypeStruct((db, MLA_HEADS, KV_LORA), BF16),
        compiler_params=_cparams(2),
        name="sample_mla",
    )(page_table, cache, q3, knew)


def _s_mem_body(q_ref, kv_ref, o_ref):
    q = q_ref[0]
    head = lax.broadcasted_iota(jnp.int32, (MEM_HEADS, 1), 0)
    out = jnp.zeros((MEM_HEADS, MEM_DH), F32)
    for h in range(MEM_HEADS):
        k = _bf(kv_ref[0, :, 0, h, :])
        v = _bf(kv_ref[0, :, 1, h, :])
        s = _dot_nt(q, k) * (MEM_DH ** -0.5)
        e = jnp.exp(s - jnp.max(s, axis=1, keepdims=True))
        p = e / jnp.sum(e, axis=1, keepdims=True)
        out = jnp.where(head == h, _dot(_bf(p), v), out)
    o_ref[0] = _bf(out)


def _s_mem(mq3, cache_mem):
    db = mq3.shape[0]
    one = lambda *tail: pl.BlockSpec((1,) + tail, lambda b: (b,) + (0,) * len(tail))
    return pl.pallas_call(
        _s_mem_body,
        grid=(db,),
        in_specs=[one(MEM_HEADS, MEM_DH), one(MEM_TOKENS, 2, MEM_HEADS, MEM_DH)],
        out_specs=one(MEM_HEADS, MEM_DH),
        out_shape=jax.ShapeDtypeStruct((db, MEM_HEADS, MEM_DH), BF16),
        compiler_params=_cparams(1),
        name="sample_mem",
    )(mq3, cache_mem)


def _sample_path(x_sample, cache_nsa_kv, cache_mla, state_nsa_win, cache_mem_kv, page_table, rel_bias,
                 ln_g, ln_b, ffn_w, pw, scw, mw):
    db = x_sample.shape[0]
    n_pages = page_table.shape[1]
    past = n_pages * PAGE
    wb = state_nsa_win.shape[2]
    n_blk = (past + 1 + SEL_BLOCK - 1) // SEL_BLOCK
    ln = lambda k: (ln_g[0, k][None], ln_b[0, k][None])
    x1 = _ffn(x_sample[:, 0], *ffn_w[0], *ln(0), db)
    qn, nkv4, win, gate, qmla, row, kmla, mq = _proj(
        x1, pw, _rope_tables(jnp.full((db,), past, jnp.int32)), db, False)
    q3 = qn.reshape(db, NSA_HEADS, NSA_DK)
    g3 = gate[:, :24].reshape(db, 3, NSA_HEADS).transpose(0, 2, 1)
    cache_t = jnp.transpose(cache_nsa_kv[0], (0, 2, 3, 4, 1)).reshape(-1, 512, PAGE)
    mla_t = jnp.transpose(cache_mla[0], (0, 2, 1))
    state_t = jnp.transpose(state_nsa_win[0], (0, 2, 3, 4, 1)).reshape(db, 256, wb)
    nc = past // CMP_BLOCK
    nbk = PAGE // CMP_BLOCK
    bc = _rel_bias(rel_bias, past - CMP_BLOCK * jnp.arange(nc) - (CMP_BLOCK - 1))
    bc = bc.reshape(NSA_HEADS, n_pages, nbk).transpose(0, 2, 1).reshape(NSA_HEADS, nc)
    bw = _rel_bias(rel_bias, wb - jnp.arange(wb))
    bn = _rel_bias(rel_bias, jnp.zeros((1,), jnp.int32))
    near = _rel_bias(rel_bias, PAGE - jnp.arange(PAGE))
    upper = jnp.arange(PAGE) >= SEL_BLOCK
    ta = jnp.where(upper, near, 0.0)
    tb = jnp.where(upper, 0.0, near)
    cmp = _s_compress(page_table, cache_t, scw)
    o8, imp = _s_cmp_win(cmp, q3, g3, state_t, win.reshape(db, 1, 256), bc, bw, bn)
    imp = imp.reshape(db, NSA_GROUPS, nbk, n_pages).transpose(0, 1, 3, 2).reshape(db, NSA_GROUPS, nc)
    idx = _s_topk(imp, n_blk)
    o8 = _s_select(page_table, idx.reshape(-1, db), cache_t, q3, g3, nkv4.reshape(db, 1, 512), o8, ta, tb, bn)
    o_nsa = o8.reshape(db, NSA_HEADS, NSA_GROUPS, NSA_DK)[:, jnp.arange(NSA_HEADS),
                                                           jnp.arange(NSA_HEADS) // NSA_HPG]
    o_nsa = o_nsa.reshape(db, 512).astype(BF16)
    o_lat = _s_mla(page_table, mla_t, qmla.reshape(db, MLA_HEADS, 256), kmla.reshape(db, 1, 256))
    o_mem = _s_mem(mq.reshape(db, MEM_HEADS, MEM_DH), cache_mem_kv[0])
    x2 = _merge(x1, o_nsa, o_lat.reshape(db, -1), o_mem.reshape(db, -1), mw, *ln(1), db)
    y = _ffn(x2, *ffn_w[1], *ln(2), db)
    new_win_t = jnp.concatenate([state_t[:, :, 1:], win[:, :, None]], axis=2)
    new_win = jnp.transpose(new_win_t.reshape(db, 2, NSA_GROUPS, NSA_DK, wb), (0, 4, 1, 2, 3))
    return (y[:, None],
            nkv4.reshape(1, db, 1, 4, NSA_GROUPS, NSA_DK),
            row.reshape(1, db, 1, MLA_ROW),
            new_win[None])


def kernel(x_prompt, x_sample, mem_prompt, cache_nsa_kv, cache_mla, state_nsa_win, cache_mem_kv, page_table, rel_bias, ln_g, ln_b, ffn_w1, ffn_w3, ffn_w2, w_in, nsa_cmp_pos, nsa_cmp_w1, nsa_cmp_w2, mla_g_q, mla_w_uq, mla_w_qr, mla_g_kv, mla_w_uk, mla_w_uv, mem_w_kv, w_br, w_o):
    assert ffn_w1.shape[0] == 1 and x_prompt.shape[0] == 1 and x_sample.shape[1] == 1
    ffn_w = [tuple(w[0, s].astype(BF16) for w in (ffn_w1, ffn_w3, ffn_w2)) for s in range(2)]
    pw = _proj_weights(w_in[0], mla_g_q[0], mla_w_uq[0], mla_w_qr[0], mla_g_kv[0], mla_w_uk[0])
    cw = _compress_weights(nsa_cmp_pos[0], nsa_cmp_w1[0], nsa_cmp_w2[0])
    mw = _merge_weights(w_in[0], mla_w_uv[0], w_br[0], w_o[0])
    yp, p_nsa, p_mla, p_win, p_mem = _prompt_path(x_prompt, mem_prompt, rel_bias, ln_g, ln_b, ffn_w, pw,
                                                  cw, mw, mem_w_kv)
    scw = _sample_compress_weights(nsa_cmp_pos[0], nsa_cmp_w1[0], nsa_cmp_w2[0])
    ys, s_nsa, s_mla, s_win = _sample_path(x_sample, cache_nsa_kv, cache_mla, state_nsa_win, cache_mem_kv,
                                           page_table, rel_bias, ln_g, ln_b, ffn_w, pw, scw, mw)
    return (yp, ys, p_nsa, p_mla, p_win, p_mem, s_nsa, s_mla, s_win)
```

```python
import functools
import math

import jax
import jax.numpy as jnp
import numpy as np
from jax import lax
from jax.experimental import pallas as pl
from jax.experimental.pallas import tpu as pltpu

F32 = jnp.float32
BF16 = jnp.bfloat16

D_MODEL = 1024
D_FF = 2816
NSA_HEADS = 8
NSA_GROUPS = 2
NSA_HPG = 4
NSA_DK = 64
CMP_BLOCK = 32
CMP_HIDDEN = 128
SEL_BLOCK = 64
N_SEL = 16
WINDOW = 512
MLA_HEADS = 8
Q_LORA = 256
KV_LORA = 128
D_NOPE = 64
D_ROPE = 32
D_V = 64
ROPE_THETA = 10000.0
MLA_ROW = KV_LORA + D_ROPE
MEM_TOKENS = 256
MEM_HEADS = 4
MEM_DH = 128
N_BRANCH = 3
BRANCH_W = 512
N_BUCKETS = 32
MAX_DISTANCE = 128
PAGE = 128
ALPHA = 2.0 ** 0.25
LN_EPS = 1e-5
RMS_EPS = 1e-6
IN_WIDTHS = (512, 768, 24, 256, 128, 32, 512, 3072)

LANES = 128
FF_CHUNK = 256
NEG = -1e30
M_INIT = -1e29
VMEM_LIMIT = 56 * 1024 * 1024


def _cparams(n_axes):
    return pltpu.CompilerParams(dimension_semantics=("arbitrary",) * n_axes,
                                vmem_limit_bytes=VMEM_LIMIT)


def _full(shape):
    n = len(shape)
    return pl.BlockSpec(shape, lambda *_: (0,) * n)


def _dot(a, b):
    return jnp.dot(a, b, preferred_element_type=F32)


def _dot_nt(a, b):
    return lax.dot_general(a, b, (((1,), (1,)), ((), ())), preferred_element_type=F32)


def _layer_norm(y, g, b):
    mu = jnp.mean(y, axis=-1, keepdims=True)
    yc = y - mu
    var = jnp.mean(yc * yc, axis=-1, keepdims=True)
    return yc * lax.rsqrt(var + LN_EPS) * g + b


def _ffn_body(x_ref, w1_ref, w3_ref, w2_ref, g_ref, b_ref, o_ref):
    x = x_ref[...]
    xb = x.astype(BF16)
    acc = jnp.zeros(x.shape, F32)
    for c in range(D_FF // FF_CHUNK):
        sl = slice(c * FF_CHUNK, (c + 1) * FF_CHUNK)
        a = _dot(xb, w1_ref[:, sl])
        b = _dot(xb, w3_ref[:, sl])
        h = (a * jax.nn.sigmoid(a) * b).astype(BF16)
        acc = acc + _dot(h, w2_ref[sl, :])
    o_ref[...] = _layer_norm(ALPHA * x + 0.5 * acc, g_ref[...], b_ref[...])


def _ffn(x, w1, w3, w2, g, b, tm):
    rows = x.shape[0]
    return pl.pallas_call(
        _ffn_body,
        grid=(rows // tm,),
        in_specs=[pl.BlockSpec((tm, D_MODEL), lambda i: (i, 0)),
                  _full(w1.shape), _full(w3.shape), _full(w2.shape), _full(g.shape), _full(b.shape)],
        out_specs=pl.BlockSpec((tm, D_MODEL), lambda i: (i, 0)),
        out_shape=jax.ShapeDtypeStruct((rows, D_MODEL), F32),
        compiler_params=_cparams(1),
        name="ffn",
    )(x, w1, w3, w2, g, b)


P_NQ, P_NKV, P_NG, P_QD, P_KVD, P_KR, P_MQ = 0, 512, 1280, 1408, 1664, 1792, 1920
P_WIDTH = 2432


def _rope(x, cos, s_lo, s_hi):
    return (x * cos + pltpu.roll(x, LANES - D_ROPE // 2, 1) * s_lo
            + pltpu.roll(x, D_ROPE // 2, 1) * s_hi)


def _rms(x, g):
    return x * lax.rsqrt(jnp.mean(x * x, axis=-1, keepdims=True) + RMS_EPS) * g


def _proj_body(prompt, x_ref, wp_ref, wuq_ref, wuk_ref, wqr_ref, gq_ref, gkv_ref,
               cos_ref, slo_ref, shi_ref, *outs):
    if prompt:
        (qn_ref, nkv4_ref, win_ref, gate_ref, qmla_ref, row_ref, kmla_ref, mq_ref,
         kslc_ref, vslct_ref, kwin_ref, vwint_ref, ckvt_ref) = outs
    else:
        qn_ref, nkv4_ref, win_ref, gate_ref, qmla_ref, row_ref, kmla_ref, mq_ref = outs
    xb = x_ref[...].astype(BF16)
    cos, s_lo, s_hi = cos_ref[...], slo_ref[...], shi_ref[...]

    def seg(start, width):
        return _dot(xb, wp_ref[:, start:start + width])

    hq = (seg(P_NQ, 512) * (NSA_DK ** -0.5)).astype(BF16)
    if prompt:
        for h in range(NSA_HEADS):
            qn_ref[h] = hq[:, h * NSA_DK:(h + 1) * NSA_DK]
    else:
        qn_ref[...] = hq
    nkv = seg(P_NKV, 768)
    nkv4_ref[...] = nkv[:, :512]
    win_ref[...] = nkv[:, 512:768]
    if prompt:
        tm = nkv.shape[0]
        nkvb = nkv.astype(BF16)
        for g in range(NSA_GROUPS):
            kslc_ref[g] = nkvb[:, 256 + g * NSA_DK:256 + (g + 1) * NSA_DK]
            kwin_ref[g] = nkvb[:, 512 + g * NSA_DK:512 + (g + 1) * NSA_DK]
        vt = nkv[:, 384:512].T.astype(BF16)
        for c in range(tm // 256):
            vslct_ref[c] = vt[:, c * 256:(c + 1) * 256]
        wt = nkv[:, 640:768].T.astype(BF16)
        for c in range(tm // 128):
            vwint_ref[c] = wt[:, c * 128:(c + 1) * 128]
    gate_ref[...] = jax.nn.sigmoid(seg(P_NG, 128))
    cq = _rms(seg(P_QD, 256), gq_ref[...]).astype(BF16)
    q_nope = _dot(cq, wuq_ref[...]).astype(BF16)
    q_lat = _dot(q_nope, wuk_ref[...])
    q_rope = _dot(cq, wqr_ref[...])
    for h in range(MLA_HEADS):
        sl = slice(h * LANES, (h + 1) * LANES)
        qmla_ref[:, 2 * h * LANES:(2 * h + 1) * LANES] = q_lat[:, sl].astype(BF16)
        qmla_ref[:, (2 * h + 1) * LANES:(2 * h + 2) * LANES] = _rope(q_rope[:, sl], cos, s_lo, s_hi).astype(BF16)
    ckv = _rms(seg(P_KVD, 128), gkv_ref[...])
    kr = _rope(seg(P_KR, 128), cos, s_lo, s_hi)
    row_ref[:, :KV_LORA] = ckv
    row_ref[:, KV_LORA:] = kr[:, :D_ROPE]
    kmla_ref[:, :KV_LORA] = ckv.astype(BF16)
    kmla_ref[:, KV_LORA:] = kr.astype(BF16)
    if prompt:
        ckvt = ckv.T.astype(BF16)
        for c in range(ckvt.shape[1] // MLA_KC):
            ckvt_ref[c] = ckvt[:, c * MLA_KC:(c + 1) * MLA_KC]
    mq_ref[...] = seg(P_MQ, 512).astype(BF16)


def _proj(x, wts, tables, tm, prompt):
    rows = x.shape[0]
    wp, wuq, wuk, wqr, gq, gkv = wts
    cos, s_lo, s_hi = tables
    row_spec = lambda w: pl.BlockSpec((tm, w), lambda i: (i, 0))
    out_shapes = [
        jax.ShapeDtypeStruct((NSA_HEADS, rows, NSA_DK) if prompt else (rows, 512), BF16),
        jax.ShapeDtypeStruct((rows, 512), F32),
        jax.ShapeDtypeStruct((rows, 256), F32),
        jax.ShapeDtypeStruct((rows, LANES), F32),
        jax.ShapeDtypeStruct((rows, 2048), BF16),
        jax.ShapeDtypeStruct((rows, MLA_ROW), F32),
        jax.ShapeDtypeStruct((rows, 256), BF16),
        jax.ShapeDtypeStruct((rows, 512), BF16),
    ]
    out_specs = [
        pl.BlockSpec((NSA_HEADS, tm, NSA_DK), lambda i: (0, i, 0)) if prompt else row_spec(512),
        row_spec(512), row_spec(256), row_spec(LANES), row_spec(2048), row_spec(MLA_ROW),
        row_spec(256), row_spec(512),
    ]
    if prompt:
        out_shapes += [
            jax.ShapeDtypeStruct((NSA_GROUPS, rows, NSA_DK), BF16),
            jax.ShapeDtypeStruct((rows // 256, 128, 256), BF16),
            jax.ShapeDtypeStruct((NSA_GROUPS, rows, NSA_DK), BF16),
            jax.ShapeDtypeStruct((rows // 128, 128, 128), BF16),
            jax.ShapeDtypeStruct((rows // MLA_KC, KV_LORA, MLA_KC), BF16),
        ]
        out_specs += [
            pl.BlockSpec((NSA_GROUPS, tm, NSA_DK), lambda i: (0, i, 0)),
            pl.BlockSpec((tm // 256, 128, 256), lambda i: (i, 0, 0)),
            pl.BlockSpec((NSA_GROUPS, tm, NSA_DK), lambda i: (0, i, 0)),
            pl.BlockSpec((tm // 128, 128, 128), lambda i: (i, 0, 0)),
            pl.BlockSpec((tm // MLA_KC, KV_LORA, MLA_KC), lambda i: (i, 0, 0)),
        ]
    return pl.pallas_call(
        functools.partial(_proj_body, prompt),
        grid=(rows // tm,),
        in_specs=[row_spec(D_MODEL), _full(wp.shape), _full(wuq.shape), _full(wuk.shape),
                  _full(wqr.shape), _full(gq.shape), _full(gkv.shape),
                  row_spec(LANES), row_spec(LANES), row_spec(LANES)],
        out_specs=out_specs,
        out_shape=out_shapes,
        compiler_params=_cparams(1),
        name="proj",
    )(x, wp, wuq, wuk, wqr, gq, gkv, cos, s_lo, s_hi)


def _rope_tables(pos):
    half = D_ROPE // 2
    freq = ROPE_THETA ** (-jnp.arange(half, dtype=F32) / half)
    ang = pos.astype(F32)[:, None] * freq
    cos, sin = jnp.cos(ang), jnp.sin(ang)
    z = jnp.zeros((pos.shape[0], LANES - D_ROPE), F32)
    zh = jnp.zeros_like(sin)
    return (jnp.concatenate([cos, cos, z], 1), jnp.concatenate([-sin, zh, z], 1),
            jnp.concatenate([zh, sin, z], 1))


def _proj_weights(w_in, g_q, w_uq, w_qr, g_kv, w_uk):
    offs = np.cumsum((0,) + IN_WIDTHS)
    col = lambda i: w_in[:, offs[i]:offs[i + 1]]
    pad = lambda a, w: jnp.pad(a, ((0, 0), (0, w - a.shape[1])))
    ng = col(2).reshape(D_MODEL, NSA_HEADS, 3).transpose(0, 2, 1).reshape(D_MODEL, 24)
    wp = jnp.concatenate([col(0), col(1), pad(ng, 128), col(3), col(4), pad(col(5), 128), col(6)], 1)
    wuq = w_uq.reshape(Q_LORA, MLA_HEADS * D_NOPE)
    eye = jnp.eye(MLA_HEADS, dtype=F32)
    wuk = jnp.einsum('rhd,hg->hdgr', w_uk, eye).reshape(MLA_HEADS * D_NOPE, MLA_HEADS * KV_LORA)
    wqr = jnp.pad(w_qr, ((0, 0), (0, 0), (0, LANES - D_ROPE))).reshape(Q_LORA, MLA_HEADS * LANES)
    return (wp.astype(BF16), wuq.astype(BF16), wuk.astype(BF16), wqr.astype(BF16),
            g_q.reshape(1, Q_LORA), g_kv.reshape(1, KV_LORA))


def _compress_weights(cmp_pos, cmp_w1, cmp_w2):
    eye = jnp.eye(NSA_GROUPS, dtype=F32)
    w1r = cmp_w1.reshape(2, CMP_BLOCK, NSA_DK, CMP_HIDDEN)
    w1 = jnp.einsum('ktdc,gG->ktgdGc', w1r, eye).reshape(2, CMP_BLOCK, 128, 256)
    w2 = jnp.einsum('kcd,gG->kgcGd', cmp_w2, eye).reshape(2, 256, 128)
    pos = jnp.concatenate([cmp_pos, cmp_pos], axis=-1)
    return pos, w1.astype(BF16), w2.astype(BF16)


def _compress_rows(x_ref, kv, nblk, pos_ref, w1_ref, w2_ref, transposed=False):
    acc = jnp.zeros((nblk, 256), F32)
    for t in range(CMP_BLOCK):
        xt = x_ref[pl.ds(t, nblk, stride=CMP_BLOCK), :]
        acc = acc + _dot((xt + pos_ref[kv, t:t + 1, :]).astype(BF16), w1_ref[kv, t])
    h = (acc * jax.nn.sigmoid(acc)).astype(BF16)
    if transposed:
        return _dot_nt(w2_ref[kv], h)
    return _dot(h, w2_ref[kv])


def _compress_prompt_body(xk_ref, xv_ref, pos_ref, w1_ref, w2_ref, w2t_ref, kc_ref, vct_ref):
    nblk = xk_ref.shape[0] // CMP_BLOCK
    kcb = _compress_rows(xk_ref, 0, nblk, pos_ref, w1_ref, w2_ref).astype(BF16)
    vt = _compress_rows(xv_ref, 1, nblk, pos_ref, w1_ref, w2t_ref, transposed=True).astype(BF16)
    for g in range(NSA_GROUPS):
        kc_ref[g] = kcb[:, g * NSA_DK:(g + 1) * NSA_DK]
        vct_ref[g] = vt[g * NSA_DK:(g + 1) * NSA_DK, :]


def _compress_prompt(nkv4, cw):
    t = nkv4.shape[0]
    rows = min(t, 4096)
    nblk, nc = rows // CMP_BLOCK, t // CMP_BLOCK
    pos, w1, w2 = cw
    w2t = w2.transpose(0, 2, 1)
    return pl.pallas_call(
        _compress_prompt_body,
        grid=(t // rows,),
        in_specs=[pl.BlockSpec((rows, 128), lambda i: (i, 0)), pl.BlockSpec((rows, 128), lambda i: (i, 1)),
                  _full(pos.shape), _full(w1.shape), _full(w2.shape), _full(w2t.shape)],
        out_specs=[pl.BlockSpec((NSA_GROUPS, nblk, NSA_DK), lambda i: (0, i, 0)),
                   pl.BlockSpec((NSA_GROUPS, NSA_DK, nblk), lambda i: (0, 0, i))],
        out_shape=[jax.ShapeDtypeStruct((NSA_GROUPS, nc, NSA_DK), BF16),
                   jax.ShapeDtypeStruct((NSA_GROUPS, NSA_DK, nc), BF16)],
        compiler_params=_cparams(1),
        name="compress_prompt",
    )(nkv4, nkv4, pos, w1, w2, w2t)


def _t5_bucket(dist):
    n = jnp.maximum(dist, 0)
    max_exact = N_BUCKETS // 2
    nf = jnp.maximum(n, 1).astype(F32)
    large = max_exact + (jnp.log(nf / max_exact) / math.log(MAX_DISTANCE / max_exact)
                         * (N_BUCKETS - max_exact)).astype(jnp.int32)
    large = jnp.minimum(large, N_BUCKETS - 1)
    return jnp.where(n < max_exact, n, large)


FAR_DIST = 129


def _rel_bias(rel_bias, dist):
    b = rel_bias[_t5_bucket(dist)] - rel_bias[N_BUCKETS - 1]
    b = jnp.where((dist >= 0)[..., None], b, NEG)
    return jnp.moveaxis(b, -1, 0)


def _lanes_hq(b):
    k = b.shape[1]
    return b.reshape(NSA_GROUPS, NSA_HPG, k, 128).transpose(0, 2, 1, 3).reshape(NSA_GROUPS, k, 512)


def _prompt_bias_tables(rel_bias):
    q = jnp.arange(128)[None, :]
    k = jnp.arange(128)[:, None]
    zero = jnp.zeros((NSA_GROUPS, 128, 512), F32)
    sub = _lanes_hq(_rel_bias(rel_bias, 128 + q - k))
    diag = _lanes_hq(_rel_bias(rel_bias, q - k))
    neg = jnp.full((NSA_GROUPS, 128, 512), NEG, F32)
    anti = _lanes_hq(jnp.broadcast_to(jnp.where(k >= q, 0.0, NEG)[None], (8, 128, 128)))
    near = jnp.stack([zero, sub, diag, neg, anti], axis=1)
    r = jnp.arange(16)[:, None]
    cmpw = jnp.stack([_lanes_hq(_rel_bias(rel_bias, q - CMP_BLOCK * (r - off) - (CMP_BLOCK - 1)))
                      for off in (4, 8, 0)], axis=1)
    return near, cmpw


T_ZERO, T_SUB, T_DIAG, T_NEG, T_ANTI = range(5)


def _softmax_update(s, vt, m_old, l_old, acc_old, scale=None):
    m_new = jnp.maximum(m_old, jnp.max(s, axis=0, keepdims=True))
    if scale is None:
        alpha = jnp.exp(m_old - m_new)
        e = jnp.exp(s - m_new)
    else:
        c = scale * math.log2(math.e)
        alpha = jnp.exp2((m_old - m_new) * c)
        e = jnp.exp2((s - m_new) * c)
    return (m_new, alpha * l_old + jnp.sum(e, axis=0, keepdims=True),
            alpha * acc_old + _dot(vt, e.astype(BF16)))


QK_AHEAD = 2


def _pipelined(items, qk, finish, carry, scale=None):
    carry = list(carry)
    raw = [qk(it) for it in items[:QK_AHEAD]]
    for n, item in enumerate(items):
        if n + QK_AHEAD < len(items):
            raw.append(qk(items[n + QK_AHEAD]))
        g = item[0]
        carry[g] = _softmax_update(*finish(item, raw[n]), *carry[g], scale=scale)
    return tuple(carry)


def _topk_rows(score, n_sel):
    nb = score.shape[0]
    blk = lax.broadcasted_iota(jnp.int32, score.shape, 0)
    sel = jnp.zeros(score.shape, F32)
    picks = []
    for _ in range(n_sel):
        m = jnp.max(score, axis=0, keepdims=True)
        j = jnp.min(jnp.where(score == m, blk, nb), axis=0, keepdims=True)
        hit = blk == j
        sel = jnp.where(hit, 1.0, sel)
        score = jnp.where(hit, -2.0, score)
        picks.append(j)
    return sel, picks


def _nsa_prompt_body(q_ref, gate_ref, kc_ref, vct_ref, kslc_ref, vslct_ref, kwin_ref, vwint_ref,
                     near_ref, cmpw_ref, o_ref, s_ref, imp_ref, sel_ref, m_ref, l_ref, acc_ref, out_ref):
    i = pl.program_id(0)
    nc = kc_ref.shape[1]
    nb = nc // 2
    gate_t = gate_ref[...].T
    lane_q = lax.broadcasted_iota(jnp.int32, (1, 128), 1)
    cur = 2 * i + (lane_q >= SEL_BLOCK).astype(jnp.int32)
    odd = i % 2
    w0 = pl.multiple_of(jnp.where(i == 0, 0, jnp.where(odd == 1, 4 * i - 4, 4 * i - 8)), 8)
    var = jnp.where(i == 0, 2, jnp.where(odd == 1, 0, 1))
    jl = i // 2
    even = 1 - odd

    def reset():
        m_ref[...] = jnp.full(m_ref.shape, M_INIT, F32)
        l_ref[...] = jnp.zeros(l_ref.shape, F32)
        acc_ref[...] = jnp.zeros(acc_ref.shape, F32)

    def gate_row(g, b):
        return jnp.concatenate([gate_t[b * 8 + 4 * g + h:b * 8 + 4 * g + h + 1, :]
                                for h in range(NSA_HPG)], axis=1)

    groups = range(NSA_GROUPS)

    def q_group(g):
        return q_ref[4 * g:4 * g + 4].reshape(4 * 128, NSA_DK)

    for g in groups:
        s_ref[g, pl.ds(nc, 16), :] = jnp.zeros((16, 512), F32)
        s_ref[g, pl.ds(0, nc), :] = _dot_nt(kc_ref[g], q_group(g))
        s_ref[g, pl.ds(w0, 16), :] = s_ref[g, pl.ds(w0, 16), :] + cmpw_ref[g, var]
        row_n = lax.broadcasted_iota(jnp.int32, (nc, 1), 0)
        s = jnp.where(row_n <= 4 * i + 3, s_ref[g, pl.ds(0, nc), :], NEG)
        m = jnp.maximum(jnp.max(s, axis=0, keepdims=True), M_INIT)
        e = jnp.exp(s - m)
        d = jnp.sum(e, axis=0, keepdims=True)
        p = e / jnp.where(d > 0.0, d, 1.0)
        out_ref[g] = gate_row(g, 0) * _dot(vct_ref[g], p.astype(BF16))
        imp_ref[g] = p[:, 0:128] + p[:, 128:256] + p[:, 256:384] + p[:, 384:512]
        imp = imp_ref[g, pl.ds(0, nb, stride=2), :] + imp_ref[g, pl.ds(1, nb, stride=2), :]
        blk = lax.broadcasted_iota(jnp.int32, (nb, 128), 0)
        valid = blk <= cur
        forced = valid & ((blk == 0) | (blk == cur) | (blk == cur - 1))
        score = jnp.where(forced, NSA_HPG + 1.0, jnp.where(valid, imp, -1.0))
        sel_ref[g], _ = _topk_rows(score, min(N_SEL, nb))

    def sel_qk(g, j):
        return _dot_nt(kslc_ref[g, pl.ds(pl.multiple_of(j * 256, 256), 256), :], q_group(g))

    def sel_scores(g, j, top, bot, s=None):
        if s is None:
            s = sel_qk(g, j)
        parts = []
        for b in range(4):
            row = sel_ref[g, pl.ds(4 * j + b, 1), :]
            mask = jnp.concatenate([row] * NSA_HPG, axis=1) > 0.5
            sb = s[b * SEL_BLOCK:(b + 1) * SEL_BLOCK]
            if top is not None:
                tab = near_ref[g, top if b < 2 else bot]
                sb = sb + tab[(b % 2) * SEL_BLOCK:(b % 2 + 1) * SEL_BLOCK]
            parts.append(jnp.where(mask, sb, NEG))
        return jnp.concatenate(parts, axis=0), vslct_ref[j, g * NSA_DK:(g + 1) * NSA_DK, :]

    pipelined = _pipelined

    def load_state():
        return tuple((m_ref[g], l_ref[g], acc_ref[g]) for g in groups)

    def store_state(st):
        for g in groups:
            m_ref[g], l_ref[g], acc_ref[g] = st[g]

    def sel_items(chunks):
        return [(g, j, top, bot) for (j, top, bot) in chunks for g in groups]

    sel_run = functools.partial(pipelined, qk=lambda it: sel_qk(it[0], it[1]),
                                finish=lambda it, raw: sel_scores(*it, raw))
    n_far = jnp.maximum(jl - 1, 0)
    unroll = 4
    init = tuple((jnp.full((1, 512), M_INIT, F32), jnp.zeros((1, 512), F32),
                  jnp.zeros((NSA_DK, 512), F32)) for _ in groups)
    far = lax.fori_loop(
        0, n_far // unroll,
        lambda jj, c: sel_run(sel_items([(unroll * jj + u, None, None) for u in range(unroll)]), carry=c), init)
    far = lax.fori_loop(n_far - n_far % unroll, n_far,
                        lambda j, c: sel_run(sel_items([(j, None, None)]), carry=c), far)
    store_state(far)
    last_tabs = (jl, jnp.where(even == 1, T_DIAG, T_SUB), jnp.where(even == 1, T_NEG, T_DIAG))

    @pl.when(jl >= 1)
    def _():
        prev_tabs = (jl - 1, T_ZERO, jnp.where(even == 1, T_SUB, T_ZERO))
        store_state(sel_run(sel_items([prev_tabs, last_tabs]), carry=load_state()))

    @pl.when(jl < 1)
    def _():
        store_state(sel_run(sel_items([last_tabs]), carry=load_state()))

    for g in groups:
        out_ref[g] = out_ref[g] + (gate_row(g, 1) / l_ref[g]) * acc_ref[g]

    def win_qk(it):
        g, j, _ = it
        return _dot_nt(kwin_ref[g, pl.ds(pl.multiple_of(j * 128, 128), 128), :], q_group(g))

    def win_finish(it, s):
        g, j, tab = it
        if tab is not None:
            s = s + near_ref[g, tab]
        return s, vwint_ref[j, g * NSA_DK:(g + 1) * NSA_DK, :]

    win_chunks = ((4, T_ANTI), (3, None), (2, None), (1, T_SUB), (0, T_DIAG))
    reset()

    @pl.when(i >= 4)
    def _():
        items = [(g, i - back, tab) for back, tab in win_chunks for g in groups]
        store_state(pipelined(items, win_qk, win_finish, load_state()))

    @pl.when(i < 4)
    def _():
        for back, tab in win_chunks:
            @pl.when(i >= back)
            def _(back=back, tab=tab):
                items = [(g, i - back, tab) for g in groups]
                store_state(pipelined(items, win_qk, win_finish, load_state()))

    for g in groups:
        out = out_ref[g] + (gate_row(g, 2) / l_ref[g]) * acc_ref[g]
        for h in range(NSA_HPG):
            col = (4 * g + h) * NSA_DK
            o_ref[:, col:col + NSA_DK] = out[:, h * 128:(h + 1) * 128].T.astype(BF16)


def _nsa_prompt(qn, gate, kc, vct, kslc, vslct, kwin, vwint, near, cmpw):
    t = qn.shape[1]
    nc = kc.shape[1]
    return pl.pallas_call(
        _nsa_prompt_body,
        grid=(t // 128,),
        in_specs=[pl.BlockSpec((NSA_HEADS, 128, NSA_DK), lambda i: (0, i, 0)),
                  pl.BlockSpec((128, LANES), lambda i: (i, 0)),
                  _full(kc.shape), _full(vct.shape), _full(kslc.shape), _full(vslct.shape),
                  _full(kwin.shape), _full(vwint.shape), _full(near.shape), _full(cmpw.shape)],
        out_specs=pl.BlockSpec((128, 512), lambda i: (i, 0)),
        out_shape=jax.ShapeDtypeStruct((t, 512), BF16),
        scratch_shapes=[pltpu.VMEM((NSA_GROUPS, nc + 16, 512), F32), pltpu.VMEM((NSA_GROUPS, nc, 128), F32),
                        pltpu.VMEM((NSA_GROUPS, nc // 2, 128), F32), pltpu.VMEM((NSA_GROUPS, 1, 512), F32),
                        pltpu.VMEM((NSA_GROUPS, 1, 512), F32), pltpu.VMEM((NSA_GROUPS, NSA_DK, 512), F32),
                        pltpu.VMEM((NSA_GROUPS, NSA_DK, 512), F32)],
        compiler_params=_cparams(1),
        name="nsa_prompt",
    )(qn, gate, kc, vct, kslc, vslct, kwin, vwint, near, cmpw)


MLA_TQ = 128
MLA_KC = 256
MLA_HALF_HEADS = 4
MLA_SCALE = (D_NOPE + D_ROPE) ** -0.5


def _mla_mask_tables():
    k = np.arange(128)[:, None]
    q = np.arange(128)[None, :]
    diag = np.tile(np.where(k <= q, 0.0, NEG), (1, MLA_HALF_HEADS))
    return jnp.asarray(np.stack([np.zeros_like(diag), diag, np.full_like(diag, NEG)]), F32)


M_ZERO, M_DIAG, M_NEG = range(3)


def _mla_prompt_body(q_ref, k_ref, vt_ref, tab_ref, o_ref):
    i = pl.program_id(0)
    halves = range(MLA_HEADS // MLA_HALF_HEADS)
    jl = i // 2
    even = 1 - i % 2

    def q_half(hh):
        return jnp.concatenate([q_ref[:, (hh * MLA_HALF_HEADS + h) * 256:(hh * MLA_HALF_HEADS + h + 1) * 256]
                                for h in range(MLA_HALF_HEADS)], axis=0)

    def qk(it):
        return _dot_nt(k_ref[pl.ds(pl.multiple_of(it[1] * MLA_KC, MLA_KC), MLA_KC), :], q_half(it[0]))

    def finish(it, s):
        _, j, top, bot = it
        if top is not None:
            s = s + jnp.concatenate([tab_ref[top], tab_ref[bot]], axis=0)
        return s, vt_ref[j]

    run = functools.partial(_pipelined, qk=qk, finish=finish, scale=MLA_SCALE)
    items = lambda chunks: [(hh, j, top, bot) for (j, top, bot) in chunks for hh in halves]
    lanes = MLA_HALF_HEADS * MLA_TQ
    init = tuple((jnp.full((1, lanes), M_INIT, F32), jnp.zeros((1, lanes), F32),
                  jnp.zeros((KV_LORA, lanes), F32)) for _ in halves)
    unroll = 4
    st = lax.fori_loop(0, jl // unroll,
                       lambda jj, c: run(items([(unroll * jj + u, None, None) for u in range(unroll)]), carry=c),
                       init)
    st = lax.fori_loop(jl - jl % unroll, jl, lambda j, c: run(items([(j, None, None)]), carry=c), st)
    st = run(items([(jl, jnp.where(even == 1, M_DIAG, M_ZERO), jnp.where(even == 1, M_NEG, M_DIAG))]), carry=st)
    for hh in halves:
        _, l, acc = st[hh]
        o = acc / l
        for h in range(MLA_HALF_HEADS):
            head = hh * MLA_HALF_HEADS + h
            o_ref[:, head * KV_LORA:(head + 1) * KV_LORA] = o[:, h * MLA_TQ:(h + 1) * MLA_TQ].T.astype(BF16)


def _mla_prompt(qmla, kmla, ckvt):
    t = qmla.shape[0]
    tabs = _mla_mask_tables()
    return pl.pallas_call(
        _mla_prompt_body,
        grid=(t // MLA_TQ,),
        in_specs=[pl.BlockSpec((MLA_TQ, 2048), lambda i: (i, 0)), _full(kmla.shape), _full(ckvt.shape),
                  _full(tabs.shape)],
        out_specs=pl.BlockSpec((MLA_TQ, MLA_HEADS * KV_LORA), lambda i: (i, 0)),
        out_shape=jax.ShapeDtypeStruct((t, MLA_HEADS * KV_LORA), BF16),
        compiler_params=_cparams(1),
        name="mla_prompt",
    )(qmla, kmla, ckvt, tabs)


def _mem_kv_body(x_ref, w_ref, o_ref, ob_ref):
    kv = _dot(x_ref[...].astype(BF16), w_ref[...])
    o_ref[...] = kv
    ob_ref[...] = kv.astype(BF16)


def _mem_kv(mem, w):
    shp = (MEM_TOKENS, 2 * MEM_HEADS * MEM_DH)
    return pl.pallas_call(
        _mem_kv_body,
        in_specs=[_full(mem.shape), _full(w.shape)],
        out_specs=[_full(shp), _full(shp)],
        out_shape=[jax.ShapeDtypeStruct(shp, F32), jax.ShapeDtypeStruct(shp, BF16)],
        grid=(1,),
        compiler_params=_cparams(1),
        name="mem_kv",
    )(mem, w)


def _mem_attn_body(q_ref, kv_ref, o_ref):
    for h in range(MEM_HEADS):
        sl = slice(h * MEM_DH, (h + 1) * MEM_DH)
        k = kv_ref[:, sl]
        v = kv_ref[:, MEM_HEADS * MEM_DH + h * MEM_DH:MEM_HEADS * MEM_DH + (h + 1) * MEM_DH]
        s = _dot_nt(q_ref[:, sl], k) * (MEM_DH ** -0.5)
        e = jnp.exp(s - jnp.max(s, axis=1, keepdims=True))
        p = e / jnp.sum(e, axis=1, keepdims=True)
        o_ref[:, sl] = _dot(p.astype(BF16), v).astype(BF16)


def _mem_attn(mq, kvb, tm):
    t = mq.shape[0]
    return pl.pallas_call(
        _mem_attn_body,
        grid=(t // tm,),
        in_specs=[pl.BlockSpec((tm, 512), lambda i: (i, 0)), _full(kvb.shape)],
        out_specs=pl.BlockSpec((tm, 512), lambda i: (i, 0)),
        out_shape=jax.ShapeDtypeStruct((t, 512), BF16),
        compiler_params=_cparams(1),
        name="mem_attn",
    )(mq, kvb)


def _merge_weights(w_in, w_uv, w_br, w_o):
    wmg = w_in[:, sum(IN_WIDTHS[:-1]):]
    eye = jnp.eye(MLA_HEADS, dtype=F32)
    wuv = jnp.einsum('rhd,hg->hrgd', w_uv, eye).reshape(MLA_HEADS * KV_LORA, MLA_HEADS * D_V)
    return wmg.astype(BF16), wuv.astype(BF16), w_br.astype(BF16), w_o.astype(BF16)


def _merge_body(x_ref, onsa_ref, olat_ref, omem_ref, wmg_ref, wuv_ref, wbr_ref, wo_ref, g_ref, b_ref,
                o_ref):
    x = x_ref[...]
    xb = x.astype(BF16)
    v_mla = _dot(olat_ref[...], wuv_ref[...]).astype(BF16)
    tot = jnp.zeros(x.shape, F32)
    for b, br in enumerate((onsa_ref[...], v_mla, omem_ref[...])):
        gate = jax.nn.sigmoid(_dot(xb, wmg_ref[:, b * D_MODEL:(b + 1) * D_MODEL]))
        tot = tot + gate * _dot(br, wbr_ref[b])
    mix = _dot(tot.astype(BF16), wo_ref[...])
    o_ref[...] = _layer_norm(ALPHA * x + mix, g_ref[...], b_ref[...])


def _merge(x, onsa, olat, omem, mw, g, b, tm):
    rows = x.shape[0]
    wmg, wuv, wbr, wo = mw
    row_spec = lambda w: pl.BlockSpec((tm, w), lambda i: (i, 0))
    return pl.pallas_call(
        _merge_body,
        grid=(rows // tm,),
        in_specs=[row_spec(D_MODEL), row_spec(512), row_spec(1024), row_spec(512),
                  _full(wmg.shape), _full(wuv.shape), _full(wbr.shape), _full(wo.shape),
                  _full(g.shape), _full(b.shape)],
        out_specs=row_spec(D_MODEL),
        out_shape=jax.ShapeDtypeStruct((rows, D_MODEL), F32),
        compiler_params=_cparams(1),
        name="merge",
    )(x, onsa, olat, omem, mw[0], wuv, wbr, wo, g, b)


def _prompt_path(x_prompt, mem_prompt, rel_bias, ln_g, ln_b, ffn_w, pw, cw, mw, mem_w_kv):
    t = x_prompt.shape[1]
    tm = 512
    ln = lambda k: (ln_g[0, k][None], ln_b[0, k][None])
    x1 = _ffn(x_prompt[0], *ffn_w[0], *ln(0), tm)
    (qn, nkv4, win, gate, qmla, row, kmla, mq, kslc, vslct, kwin, vwint, ckvt) = _proj(
        x1, pw, _rope_tables(jnp.arange(t)), tm, True)
    kc, vct = _compress_prompt(nkv4, cw)
    near, cmpw = _prompt_bias_tables(rel_bias)
    o_nsa = _nsa_prompt(qn, gate, kc, vct, kslc, vslct, kwin, vwint, near, cmpw)
    o_lat = _mla_prompt(qmla, kmla, ckvt)
    mem_kv, mem_kvb = _mem_kv(mem_prompt[0], mem_w_kv[0].reshape(D_MODEL, -1).astype(BF16))
    o_mem = _mem_attn(mq, mem_kvb, tm)
    x2 = _merge(x1, o_nsa, o_lat, o_mem, mw, *ln(1), tm)
    y = _ffn(x2, *ffn_w[1], *ln(2), tm)
    wb = min(WINDOW, t)
    return (y[None],
            nkv4.reshape(1, 1, t, 4, NSA_GROUPS, NSA_DK),
            row.reshape(1, 1, t, MLA_ROW),
            win[t - wb:].reshape(1, 1, wb, 2, NSA_GROUPS, NSA_DK),
            mem_kv.reshape(1, 1, MEM_TOKENS, 2, MEM_HEADS, MEM_DH))


def _own_group_lanes(q):
    q2 = jnp.concatenate([q, q], axis=1)
    head = lax.broadcasted_iota(jnp.int32, q2.shape, 0)
    lane = lax.broadcasted_iota(jnp.int32, q2.shape, 1)
    return jnp.where(head // NSA_HPG == lane // NSA_DK, q2, jnp.zeros_like(q2))


def _bf(x):
    return x.astype(BF16)


def _new_key_score(qm, k_new):
    return jnp.sum(qm.astype(F32) * _bf(k_new).astype(F32), axis=1, keepdims=True)


CMP_PITCH = 136


def _sample_compress_weights(cmp_pos, cmp_w1, cmp_w2):
    eye4 = jnp.eye(PAGE // CMP_BLOCK, dtype=F32)
    w1p = cmp_w1.reshape(2, CMP_BLOCK, NSA_DK // 2, 2, CMP_HIDDEN)
    w1 = jnp.einsum('ktpjc,nm->kpjntmc', w1p, eye4).reshape(2, NSA_DK // 2, 256, 512)
    eye2 = jnp.eye(NSA_GROUPS, dtype=F32)
    w2 = jnp.einsum('kcd,gG->kgcGd', cmp_w2, eye2).reshape(2, 256, 128)
    pos = jnp.tile(cmp_pos.transpose(0, 2, 1), (1, 1, PAGE // CMP_BLOCK))
    return pos, w1.astype(BF16), w2.astype(BF16)


def _s_compress_body(pt_ref, cache_ref, pos_ref, w1_ref, w2_ref, out_ref, buf, sem):
    kv, b = pl.program_id(0), pl.program_id(1)
    db = pl.num_programs(1)
    n_pages = pt_ref.shape[1]
    n = kv * db + b
    slot = n % 2

    def copies(step, sl, wait):
        skv, sb = step // db, step % db

        for p in range(n_pages):
            cp = pltpu.make_async_copy(
                cache_ref.at[pt_ref[sb, p], pl.ds(pl.multiple_of(skv * 128, 128), 128), :],
                buf.at[sl, pl.ds(p * CMP_PITCH, 128), :], sem.at[sl])
            cp.wait() if wait else cp.start()

    @pl.when(n == 0)
    def _():
        copies(0, 0, False)

    def compress():
        page_rows = buf.at[slot]
        acc = jnp.zeros((NSA_GROUPS * n_pages, 512), F32)
        for dp in range(NSA_DK // 2):
            rows = []
            for g in range(NSA_GROUPS):
                pair = [page_rows[pl.ds(g * NSA_DK + 2 * dp + j, n_pages, stride=CMP_PITCH), :]
                        + pos_ref[0, 2 * dp + j:2 * dp + j + 1, :] for j in range(2)]
                rows.append(jnp.concatenate(pair, axis=1))
            acc = acc + _dot(_bf(jnp.concatenate(rows, axis=0)), w1_ref[0, dp])
        h = _bf(acc * jax.nn.sigmoid(acc))
        for nb in range(PAGE // CMP_BLOCK):
            hh = jnp.concatenate([h[g * n_pages:(g + 1) * n_pages, nb * 128:(nb + 1) * 128]
                                  for g in range(NSA_GROUPS)], axis=1)
            out_ref[0, 0, nb] = _bf(_dot(hh, w2_ref[0]))

    @pl.when(n + 1 < 2 * db)
    def _():
        copies(n, slot, True)
        copies(n + 1, 1 - slot, False)
        compress()

    @pl.when(n + 1 >= 2 * db)
    def _():
        copies(n, slot, True)
        compress()


def _s_compress(page_table, cache_t, cw):
    db, n_pages = page_table.shape
    pos, w1, w2 = cw
    kvspec = lambda shape: pl.BlockSpec((1,) + shape[1:], lambda kv, b, pt: (kv,) + (0,) * (len(shape) - 1))
    nbk = PAGE // CMP_BLOCK
    return pl.pallas_call(
        _s_compress_body,
        grid_spec=pltpu.PrefetchScalarGridSpec(
            num_scalar_prefetch=1,
            grid=(2, db),
            in_specs=[pl.BlockSpec(memory_space=pl.ANY), kvspec(pos.shape), kvspec(w1.shape), kvspec(w2.shape)],
            out_specs=pl.BlockSpec((1, 1, nbk, n_pages, 128), lambda kv, b, pt: (kv, b, 0, 0, 0)),
            scratch_shapes=[pltpu.VMEM((2, n_pages * CMP_PITCH, 128), F32), pltpu.SemaphoreType.DMA((2,))],
        ),
        out_shape=jax.ShapeDtypeStruct((2, db, nbk, n_pages, 128), BF16),
        compiler_params=_cparams(2),
        name="sample_compress",
    )(page_table, cache_t, pos, w1, w2)


def _s_cmp_win_body(cmp_ref, q_ref, g_ref, st_ref, wnew_ref, bc_ref, bw_ref, bn_ref, o8_ref, imp_ref):
    nc = bc_ref.shape[1]
    kc = cmp_ref[0, 0].reshape(nc, 128)
    vc = cmp_ref[1, 0].reshape(nc, 128)
    qm = _own_group_lanes(q_ref[0])
    gate = g_ref[0]
    s = _dot_nt(qm, kc) + bc_ref[...]
    e = jnp.exp(s - jnp.max(s, axis=1, keepdims=True))
    p = e / jnp.sum(e, axis=1, keepdims=True)
    out = gate[:, 0:1] * _dot(_bf(p), vc)
    for g in range(NSA_GROUPS):
        imp_ref[0, g:g + 1, :] = jnp.sum(p[NSA_HPG * g:NSA_HPG * (g + 1)], axis=0, keepdims=True)
    st = st_ref[0]
    wnew = wnew_ref[0]
    s = _dot(qm, _bf(st[:128])) + bw_ref[...]
    s_new = _new_key_score(qm, wnew[:, :128]) + bn_ref[...]
    m = jnp.maximum(jnp.max(s, axis=1, keepdims=True), s_new)
    e = jnp.exp(s - m)
    e_new = jnp.exp(s_new - m)
    l = jnp.sum(e, axis=1, keepdims=True) + e_new
    acc = _dot_nt(_bf(e), _bf(st[128:])) + _bf(e_new).astype(F32) * _bf(wnew[:, 128:]).astype(F32)
    o8_ref[0] = out + (gate[:, 2:3] / l) * acc


def _s_cmp_win(cmp, q3, g3, state_t, wnew, bc, bw, bn):
    db = q3.shape[0]
    nc = bc.shape[1]
    wb = state_t.shape[2]
    one = lambda *tail: pl.BlockSpec((1,) + tail, lambda b: (b,) + (0,) * len(tail))
    return pl.pallas_call(
        _s_cmp_win_body,
        grid=(db,),
        in_specs=[pl.BlockSpec((2, 1) + cmp.shape[2:], lambda b: (0, b, 0, 0, 0)),
                  one(NSA_HEADS, NSA_DK), one(NSA_HEADS, 3), one(256, wb), one(1, 256),
                  _full(bc.shape), _full(bw.shape), _full(bn.shape)],
        out_specs=[one(NSA_HEADS, 128), one(NSA_GROUPS, nc)],
        out_shape=[jax.ShapeDtypeStruct((db, NSA_HEADS, 128), F32),
                   jax.ShapeDtypeStruct((db, NSA_GROUPS, nc), F32)],
        compiler_params=_cparams(1),
        name="sample_cmp_win",
    )(cmp, q3, g3, state_t, wnew, bc, bw, bn)


def _s_topk_body(n_blk, imp_ref, idx_ref, t_ref):
    db, nc = imp_ref.shape[0], imp_ref.shape[2]
    nbp = nc // 2
    rows = t_ref.shape[0] // 2
    cur = n_blk - 1
    for g in range(NSA_GROUPS):
        t_ref[pl.ds(0, nc), :] = imp_ref[:, g, :].T
        t_ref[pl.ds(nc, t_ref.shape[0] - nc), :] = jnp.zeros((t_ref.shape[0] - nc, db), F32)
        imp = t_ref[pl.ds(0, rows, stride=2), :] + t_ref[pl.ds(1, rows, stride=2), :]
        blk = lax.broadcasted_iota(jnp.int32, (rows, db), 0)
        forced = (blk == 0) | (blk == cur) | (blk == cur - 1)
        score = jnp.where(blk >= n_blk, -2.0, jnp.where(forced, NSA_HPG + 1.0, imp))
        _, picks = _topk_rows(score, min(N_SEL, n_blk))
        idx_ref[g] = jnp.concatenate(picks, axis=0)


def _s_topk(imp, n_blk):
    db, _, nc = imp.shape
    rows = -(-n_blk // 8) * 8
    n_sel = min(N_SEL, n_blk)
    return pl.pallas_call(
        functools.partial(_s_topk_body, n_blk),
        grid=(1,),
        in_specs=[_full(imp.shape)],
        out_specs=_full((NSA_GROUPS, n_sel, db)),
        out_shape=jax.ShapeDtypeStruct((NSA_GROUPS, n_sel, db), jnp.int32),
        scratch_shapes=[pltpu.VMEM((2 * rows, db), F32)],
        compiler_params=_cparams(1),
        name="sample_topk",
    )(imp)


def _s_select_body(n_sel, pt_ref, idx_ref, cache_ref, q_ref, g_ref, new_ref, o8in_ref, ta_ref, tb_ref,
                   bn_ref, o8_ref, buf, sem):
    b = pl.program_id(0)
    nbp = pt_ref.shape[1] * (PAGE // SEL_BLOCK)
    slot = b % 2
    n_dma = NSA_GROUPS * n_sel

    def copy(seq, sl, n):
        blk = jnp.minimum(idx_ref[n, seq], nbp - 1)
        return pltpu.make_async_copy(cache_ref.at[pt_ref[seq, blk // 2], pl.ds(256, 256), :],
                                     buf.at[sl, n // n_sel, :, pl.ds((n % n_sel) * PAGE, PAGE)], sem.at[sl])

    def start(seq, sl):
        for n in range(n_dma):
            copy(seq, sl, n).start()

    @pl.when(b == 0)
    def _():
        start(0, 0)

    @pl.when(b + 1 < pl.num_programs(0))
    def _():
        start(b + 1, 1 - slot)

    for n in range(n_dma):
        copy(b, slot, n).wait()
    qm = _own_group_lanes(q_ref[0])
    gate = g_ref[0]
    new = new_ref[0]
    head = lax.broadcasted_iota(jnp.int32, (NSA_HEADS, 1), 0)
    upper = (lax.broadcasted_iota(jnp.int32, (1, PAGE), 1) >= SEL_BLOCK).astype(F32)
    s_new = _new_key_score(qm, new[:, 256:384]) + bn_ref[...]
    v_new = _bf(new[:, 384:512]).astype(F32)
    out = o8in_ref[0]
    for g in range(NSA_GROUPS):
        pieces = []
        for r in range(n_sel):
            blk = idx_ref[g * n_sel + r, b]
            wa = jnp.where(blk == nbp - 1, 1.0, 0.0)
            wb = jnp.where(blk == nbp - 2, 1.0, 0.0)
            wm = jnp.where(blk >= nbp, NEG, 0.0)
            hb = (jnp.minimum(blk, nbp - 1) % 2).astype(F32)
            other_half = upper + hb - 2.0 * upper * hb
            pieces.append(wa * ta_ref[...] + wb * tb_ref[...] + wm + other_half * NEG)
        s = _dot(qm, _bf(buf[slot, g, pl.ds(0, 128), :])) + jnp.concatenate(pieces, axis=1)
        m = jnp.maximum(jnp.max(s, axis=1, keepdims=True), s_new)
        e = jnp.exp(s - m)
        e_new = jnp.exp(s_new - m)
        l = jnp.sum(e, axis=1, keepdims=True) + e_new
        acc = _dot_nt(_bf(e), _bf(buf[slot, g, pl.ds(128, 128), :])) + _bf(e_new).astype(F32) * v_new
        out = out + jnp.where(head // NSA_HPG == g, (gate[:, 1:2] / l) * acc, 0.0)
    o8_ref[0] = out


def _s_select(page_table, idx2, cache, q3, g3, new4, o8, ta, tb, bn):
    db = page_table.shape[0]
    n_sel = idx2.shape[0] // NSA_GROUPS
    one = lambda *tail: pl.BlockSpec((1,) + tail, lambda b, pt, ix: (b,) + (0,) * len(tail))
    fullp = lambda shape: pl.BlockSpec(shape, lambda b, pt, ix: (0,) * len(shape))
    return pl.pallas_call(
        functools.partial(_s_select_body, n_sel),
        grid_spec=pltpu.PrefetchScalarGridSpec(
            num_scalar_prefetch=2,
            grid=(db,),
            in_specs=[pl.BlockSpec(memory_space=pl.ANY), one(NSA_HEADS, NSA_DK), one(NSA_HEADS, 3),
                      one(1, 512), one(NSA_HEADS, 128), fullp(ta.shape), fullp(tb.shape), fullp(bn.shape)],
            out_specs=one(NSA_HEADS, 128),
            scratch_shapes=[pltpu.VMEM((2, NSA_GROUPS, 256, n_sel * PAGE), F32),
                            pltpu.SemaphoreType.DMA((2,))],
        ),
        out_shape=jax.ShapeDtypeStruct((db, NSA_HEADS, 128), F32),
        compiler_params=_cparams(1),
        name="sample_select",
    )(page_table, idx2, cache, q3, g3, new4, o8, ta, tb, bn)


MLA_S_PAGES = 32


def _s_mla_body(pt_ref, cache_ref, q_ref, knew_ref, o_ref, buf, sem, m_ref, l_ref, acc_ref):
    b, c = pl.program_id(0), pl.program_id(1)
    ncb = pl.num_programs(1)
    n_pages = buf.shape[2] // PAGE
    n = b * ncb + c
    slot = n % 2

    def copy(step, sl, p):
        return pltpu.make_async_copy(cache_ref.at[pt_ref[step // ncb, (step % ncb) * n_pages + p]],
                                     buf.at[sl, :, pl.ds(p * PAGE, PAGE)], sem.at[sl])

    def start(step, sl):
        for p in range(n_pages):
            copy(step, sl, p).start()

    @pl.when(n == 0)
    def _():
        start(0, 0)

    @pl.when(n + 1 < pl.num_programs(0) * ncb)
    def _():
        start(n + 1, 1 - slot)

    @pl.when(c == 0)
    def _():
        m_ref[...] = jnp.full(m_ref.shape, M_INIT, F32)
        l_ref[...] = jnp.zeros(l_ref.shape, F32)
        acc_ref[...] = jnp.zeros(acc_ref.shape, F32)

    for p in range(n_pages):
        copy(n, slot, p).wait()
    q = q_ref[0][:, :MLA_ROW]
    kt = _bf(buf[slot])
    s = _dot(q, kt) * MLA_SCALE
    m_old = m_ref[...]
    m_new = jnp.maximum(m_old, jnp.max(s, axis=1, keepdims=True))
    alpha = jnp.exp(m_old - m_new)
    e = jnp.exp(s - m_new)
    l_ref[...] = alpha * l_ref[...] + jnp.sum(e, axis=1, keepdims=True)
    acc_ref[...] = alpha * acc_ref[...] + _dot_nt(_bf(e), kt[:KV_LORA])
    m_ref[...] = m_new

    @pl.when(c == ncb - 1)
    def _():
        k_new = knew_ref[0]
        s_new = jnp.sum(q_ref[0].astype(F32) * k_new.astype(F32), axis=1, keepdims=True) * MLA_SCALE
        m_old = m_ref[...]
        m_new = jnp.maximum(m_old, s_new)
        alpha = jnp.exp(m_old - m_new)
        e_new = jnp.exp(s_new - m_new)
        l = alpha * l_ref[...] + e_new
        acc = alpha * acc_ref[...] + _bf(e_new).astype(F32) * k_new[:, :KV_LORA].astype(F32)
        o_ref[0] = _bf(acc / l)


def _s_mla(page_table, cache, q3, knew):
    db, n_pages = page_table.shape
    step_pages = min(MLA_S_PAGES, n_pages)
    ncb = n_pages // step_pages
    one = lambda *tail: pl.BlockSpec((1,) + tail, lambda b, c, pt: (b,) + (0,) * len(tail))
    return pl.pallas_call(
        _s_mla_body,
        grid_spec=pltpu.PrefetchScalarGridSpec(
            num_scalar_prefetch=1,
            grid=(db, ncb),
            in_specs=[pl.BlockSpec(memory_space=pl.ANY), one(MLA_HEADS, 256), one(1, 256)],
            out_specs=one(MLA_HEADS, KV_LORA),
            scratch_shapes=[pltpu.VMEM((2, MLA_ROW, step_pages * PAGE), F32),
                            pltpu.SemaphoreType.DMA((2,)), pltpu.VMEM((MLA_HEADS, 1), F32),
                            pltpu.VMEM((MLA_HEADS, 1), F32), pltpu.VMEM((MLA_HEADS, KV_LORA), F32)],
        ),
        out_shape=jax.ShapeDtypeStruct((db, MLA_HEADS, KV_LORA), BF16),
        compiler_params=_cparams(2),
        name="sample_mla",
    )(page_table, cache, q3, knew)


def _s_mem_body(q_ref, kv_ref, o_ref):
    q = q_ref[0]
    head = lax.broadcasted_iota(jnp.int32, (MEM_HEADS, 1), 0)
    out = jnp.zeros((MEM_HEADS, MEM_DH), F32)
    for h in range(MEM_HEADS):
        k = _bf(kv_ref[0, :, 0, h, :])
        v = _bf(kv_ref[0, :, 1, h, :])
        s = _dot_nt(q, k) * (MEM_DH ** -0.5)
        e = jnp.exp(s - jnp.max(s, axis=1, keepdims=True))
        p = e / jnp.sum(e, axis=1, keepdims=True)
        out = jnp.where(head == h, _dot(_bf(p), v), out)
    o_ref[0] = _bf(out)


def _s_mem(mq3, cache_mem):
    db = mq3.shape[0]
    one = lambda *tail: pl.BlockSpec((1,) + tail, lambda b: (b,) + (0,) * len(tail))
    return pl.pallas_call(
        _s_mem_body,
        grid=(db,),
        in_specs=[one(MEM_HEADS, MEM_DH), one(MEM_TOKENS, 2, MEM_HEADS, MEM_DH)],
        out_specs=one(MEM_HEADS, MEM_DH),
        out_shape=jax.ShapeDtypeStruct((db, MEM_HEADS, MEM_DH), BF16),
        compiler_params=_cparams(1),
        name="sample_mem",
    )(mq3, cache_mem)


def _sample_path(x_sample, cache_nsa_kv, cache_mla, state_nsa_win, cache_mem_kv, page_table, rel_bias,
                 ln_g, ln_b, ffn_w, pw, scw, mw):
    db = x_sample.shape[0]
    n_pages = page_table.shape[1]
    past = n_pages * PAGE
    wb = state_nsa_win.shape[2]
    n_blk = (past + 1 + SEL_BLOCK - 1) // SEL_BLOCK
    ln = lambda k: (ln_g[0, k][None], ln_b[0, k][None])
    x1 = _ffn(x_sample[:, 0], *ffn_w[0], *ln(0), db)
    qn, nkv4, win, gate, qmla, row, kmla, mq = _proj(
        x1, pw, _rope_tables(jnp.full((db,), past, jnp.int32)), db, False)
    q3 = qn.reshape(db, NSA_HEADS, NSA_DK)
    g3 = gate[:, :24].reshape(db, 3, NSA_HEADS).transpose(0, 2, 1)
    cache_t = jnp.transpose(cache_nsa_kv[0], (0, 2, 3, 4, 1)).reshape(-1, 512, PAGE)
    mla_t = jnp.transpose(cache_mla[0], (0, 2, 1))
    state_t = jnp.transpose(state_nsa_win[0], (0, 2, 3, 4, 1)).reshape(db, 256, wb)
    nc = past // CMP_BLOCK
    nbk = PAGE // CMP_BLOCK
    bc = _rel_bias(rel_bias, past - CMP_BLOCK * jnp.arange(nc) - (CMP_BLOCK - 1))
    bc = bc.reshape(NSA_HEADS, n_pages, nbk).transpose(0, 2, 1).reshape(NSA_HEADS, nc)
    bw = _rel_bias(rel_bias, wb - jnp.arange(wb))
    bn = _rel_bias(rel_bias, jnp.zeros((1,), jnp.int32))
    near = _rel_bias(rel_bias, PAGE - jnp.arange(PAGE))
    upper = jnp.arange(PAGE) >= SEL_BLOCK
    ta = jnp.where(upper, near, 0.0)
    tb = jnp.where(upper, 0.0, near)
    cmp = _s_compress(page_table, cache_t, scw)
    o8, imp = _s_cmp_win(cmp, q3, g3, state_t, win.reshape(db, 1, 256), bc, bw, bn)
    imp = imp.reshape(db, NSA_GROUPS, nbk, n_pages).transpose(0, 1, 3, 2).reshape(db, NSA_GROUPS, nc)
    idx = _s_topk(imp, n_blk)
    o8 = _s_select(page_table, idx.reshape(-1, db), cache_t, q3, g3, nkv4.reshape(db, 1, 512), o8, ta, tb, bn)
    o_nsa = o8.reshape(db, NSA_HEADS, NSA_GROUPS, NSA_DK)[:, jnp.arange(NSA_HEADS),
                                                           jnp.arange(NSA_HEADS) // NSA_HPG]
    o_nsa = o_nsa.reshape(db, 512).astype(BF16)
    o_lat = _s_mla(page_table, mla_t, qmla.reshape(db, MLA_HEADS, 256), kmla.reshape(db, 1, 256))
    o_mem = _s_mem(mq.reshape(db, MEM_HEADS, MEM_DH), cache_mem_kv[0])
    x2 = _merge(x1, o_nsa, o_lat.reshape(db, -1), o_mem.reshape(db, -1), mw, *ln(1), db)
    y = _ffn(x2, *ffn_w[1], *ln(2), db)
    new_win_t = jnp.concatenate([state_t[:, :, 1:], win[:, :, None]], axis=2)
    new_win = jnp.transpose(new_win_t.reshape(db, 2, NSA_GROUPS, NSA_DK, wb), (0, 4, 1, 2, 3))
    return (y[:, None],
            nkv4.reshape(1, db, 1, 4, NSA_GROUPS, NSA_DK),
            row.reshape(1, db, 1, MLA_ROW),
            new_win[None])


def kernel(x_prompt, x_sample, mem_prompt, cache_nsa_kv, cache_mla, state_nsa_win, cache_mem_kv, page_table, rel_bias, ln_g, ln_b, ffn_w1, ffn_w3, ffn_w2, w_in, nsa_cmp_pos, nsa_cmp_w1, nsa_cmp_w2, mla_g_q, mla_w_uq, mla_w_qr, mla_g_kv, mla_w_uk, mla_w_uv, mem_w_kv, w_br, w_o):
    assert ffn_w1.shape[0] == 1 and x_prompt.shape[0] == 1 and x_sample.shape[1] == 1
    ffn_w = [tuple(w[0, s].astype(BF16) for w in (ffn_w1, ffn_w3, ffn_w2)) for s in range(2)]
    pw = _proj_weights(w_in[0], mla_g_q[0], mla_w_uq[0], mla_w_qr[0], mla_g_kv[0], mla_w_uk[0])
    cw = _compress_weights(nsa_cmp_pos[0], nsa_cmp_w1[0], nsa_cmp_w2[0])
    mw = _merge_weights(w_in[0], mla_w_uv[0], w_br[0], w_o[0])
    yp, p_nsa, p_mla, p_win, p_mem = _prompt_path(x_prompt, mem_prompt, rel_bias, ln_g, ln_b, ffn_w, pw,
                                                  cw, mw, mem_w_kv)
    scw = _sample_compress_weights(nsa_cmp_pos[0], nsa_cmp_w1[0], nsa_cmp_w2[0])
    ys, s_nsa, s_mla, s_win = _sample_path(x_sample, cache_nsa_kv, cache_mla, state_nsa_win, cache_mem_kv,
                                           page_table, rel_bias, ln_g, ln_b, ffn_w, pw, scw, mw)
    return (yp, ys, p_nsa, p_mla, p_win, p_mem, s_nsa, s_mla, s_win)
```

```python
import functools
import math

import jax
import jax.numpy as jnp
import numpy as np
from jax import lax
from jax.experimental import pallas as pl
from jax.experimental.pallas import tpu as pltpu

F32 = jnp.float32
BF16 = jnp.bfloat16

D_MODEL = 1024
D_FF = 2816
NSA_HEADS = 8
NSA_GROUPS = 2
NSA_HPG = 4
NSA_DK = 64
CMP_BLOCK = 32
CMP_HIDDEN = 128
SEL_BLOCK = 64
N_SEL = 16
WINDOW = 512
MLA_HEADS = 8
Q_LORA = 256
KV_LORA = 128
D_NOPE = 64
D_ROPE = 32
D_V = 64
ROPE_THETA = 10000.0
MLA_ROW = KV_LORA + D_ROPE
MEM_TOKENS = 256
MEM_HEADS = 4
MEM_DH = 128
N_BRANCH = 3
BRANCH_W = 512
N_BUCKETS = 32
MAX_DISTANCE = 128
PAGE = 128
ALPHA = 2.0 ** 0.25
LN_EPS = 1e-5
RMS_EPS = 1e-6
IN_WIDTHS = (512, 768, 24, 256, 128, 32, 512, 3072)

LANES = 128
BF16_ROWS = 16
FF_CHUNK = 256
NEG = -1e30
M_INIT = -1e29
VMEM_LIMIT = 56 * 1024 * 1024


def _cparams(n_axes):
    return pltpu.CompilerParams(dimension_semantics=("arbitrary",) * n_axes,
                                vmem_limit_bytes=VMEM_LIMIT)


def _full(shape):
    n = len(shape)
    return pl.BlockSpec(shape, lambda *_: (0,) * n)


def _dot(a, b):
    return jnp.dot(a, b, preferred_element_type=F32)


def _dot_nt(a, b):
    return lax.dot_general(a, b, (((1,), (1,)), ((), ())), preferred_element_type=F32)


def _layer_norm(y, g, b):
    mu = jnp.mean(y, axis=-1, keepdims=True)
    yc = y - mu
    var = jnp.mean(yc * yc, axis=-1, keepdims=True)
    return yc * lax.rsqrt(var + LN_EPS) * g + b


def _ffn_body(x_ref, w1_ref, w3_ref, w2_ref, g_ref, b_ref, o_ref):
    x = x_ref[...]
    xb = x.astype(BF16)
    acc = jnp.zeros(x.shape, F32)
    for c in range(D_FF // FF_CHUNK):
        sl = slice(c * FF_CHUNK, (c + 1) * FF_CHUNK)
        a = _dot(xb, w1_ref[:, sl])
        b = _dot(xb, w3_ref[:, sl])
        h = (a * jax.nn.sigmoid(a) * b).astype(BF16)
        acc = acc + _dot(h, w2_ref[sl, :])
    o_ref[...] = _layer_norm(ALPHA * x + 0.5 * acc, g_ref[...], b_ref[...])


def _ffn(x, w1, w3, w2, g, b, tm):
    rows = x.shape[0]
    return pl.pallas_call(
        _ffn_body,
        grid=(rows // tm,),
        in_specs=[pl.BlockSpec((tm, D_MODEL), lambda i: (i, 0)),
                  _full(w1.shape), _full(w3.shape), _full(w2.shape), _full(g.shape), _full(b.shape)],
        out_specs=pl.BlockSpec((tm, D_MODEL), lambda i: (i, 0)),
        out_shape=jax.ShapeDtypeStruct((rows, D_MODEL), F32),
        compiler_params=_cparams(1),
        name="ffn",
    )(x, w1, w3, w2, g, b)


P_NQ, P_NKV, P_NG, P_QD, P_KVD, P_KR, P_MQ = 0, 512, 1280, 1408, 1664, 1792, 1920
P_WIDTH = 2432


def _rope(x, cos, s_lo, s_hi):
    return (x * cos + pltpu.roll(x, LANES - D_ROPE // 2, 1) * s_lo
            + pltpu.roll(x, D_ROPE // 2, 1) * s_hi)


def _rms(x, g):
    return x * lax.rsqrt(jnp.mean(x * x, axis=-1, keepdims=True) + RMS_EPS) * g


def _proj_body(prompt, x_ref, wp_ref, wuq_ref, wuk_ref, wqr_ref, gq_ref, gkv_ref,
               cos_ref, slo_ref, shi_ref, *outs):
    if prompt:
        (qn_ref, nkv4_ref, win_ref, gate_ref, qmla_ref, row_ref, kmla_ref, mq_ref,
         kslc_ref, vslct_ref, kwin_ref, vwint_ref, ckvt_ref) = outs
    else:
        qn_ref, nkv4_ref, win_ref, gate_ref, qmla_ref, row_ref, kmla_ref, mq_ref = outs
    xb = x_ref[...].astype(BF16)
    cos, s_lo, s_hi = cos_ref[...], slo_ref[...], shi_ref[...]

    def seg(start, width):
        return _dot(xb, wp_ref[:, start:start + width])

    hq = (seg(P_NQ, 512) * (NSA_DK ** -0.5)).astype(BF16)
    if prompt:
        for h in range(NSA_HEADS):
            qn_ref[h] = hq[:, h * NSA_DK:(h + 1) * NSA_DK]
    else:
        qn_ref[...] = hq
    nkv = seg(P_NKV, 768)
    nkv4_ref[...] = nkv[:, :512]
    win_ref[...] = nkv[:, 512:768]
    if prompt:
        tm = nkv.shape[0]
        nkvb = nkv.astype(BF16)
        for g in range(NSA_GROUPS):
            kslc_ref[g] = nkvb[:, 256 + g * NSA_DK:256 + (g + 1) * NSA_DK]
            kwin_ref[g] = nkvb[:, 512 + g * NSA_DK:512 + (g + 1) * NSA_DK]
        vt = nkv[:, 384:512].T.astype(BF16)
        for c in range(tm // 256):
            vslct_ref[c] = vt[:, c * 256:(c + 1) * 256]
        wt = nkv[:, 640:768].T.astype(BF16)
        for c in range(tm // 128):
            vwint_ref[c] = wt[:, c * 128:(c + 1) * 128]
    gate_ref[...] = jax.nn.sigmoid(seg(P_NG, 128))
    cq = _rms(seg(P_QD, 256), gq_ref[...]).astype(BF16)
    q_nope = _dot(cq, wuq_ref[...]).astype(BF16)
    q_lat = _dot(q_nope, wuk_ref[...])
    q_rope = _dot(cq, wqr_ref[...])
    for h in range(MLA_HEADS):
        sl = slice(h * LANES, (h + 1) * LANES)
        qmla_ref[:, 2 * h * LANES:(2 * h + 1) * LANES] = q_lat[:, sl].astype(BF16)
        qmla_ref[:, (2 * h + 1) * LANES:(2 * h + 2) * LANES] = _rope(q_rope[:, sl], cos, s_lo, s_hi).astype(BF16)
    ckv = _rms(seg(P_KVD, 128), gkv_ref[...])
    kr = _rope(seg(P_KR, 128), cos, s_lo, s_hi)
    row_ref[:, :KV_LORA] = ckv
    row_ref[:, KV_LORA:] = kr[:, :D_ROPE]
    kmla_ref[:, :KV_LORA] = ckv.astype(BF16)
    kmla_ref[:, KV_LORA:] = kr.astype(BF16)
    if prompt:
        ckvt = ckv.T.astype(BF16)
        for c in range(ckvt.shape[1] // MLA_KC):
            ckvt_ref[c] = ckvt[:, c * MLA_KC:(c + 1) * MLA_KC]
    mq_ref[...] = seg(P_MQ, 512).astype(BF16)


def _proj(x, wts, tables, tm, prompt):
    rows = x.shape[0]
    wp, wuq, wuk, wqr, gq, gkv = wts
    cos, s_lo, s_hi = tables
    row_spec = lambda w: pl.BlockSpec((tm, w), lambda i: (i, 0))
    out_shapes = [
        jax.ShapeDtypeStruct((NSA_HEADS, rows, NSA_DK) if prompt else (rows, 512), BF16),
        jax.ShapeDtypeStruct((rows, 512), F32),
        jax.ShapeDtypeStruct((rows, 256), F32),
        jax.ShapeDtypeStruct((rows, LANES), F32),
        jax.ShapeDtypeStruct((rows, 2048), BF16),
        jax.ShapeDtypeStruct((rows, MLA_ROW), F32),
        jax.ShapeDtypeStruct((rows, 256), BF16),
        jax.ShapeDtypeStruct((rows, 512), BF16),
    ]
    out_specs = [
        pl.BlockSpec((NSA_HEADS, tm, NSA_DK), lambda i: (0, i, 0)) if prompt else row_spec(512),
        row_spec(512), row_spec(256), row_spec(LANES), row_spec(2048), row_spec(MLA_ROW),
        row_spec(256), row_spec(512),
    ]
    if prompt:
        out_shapes += [
            jax.ShapeDtypeStruct((NSA_GROUPS, rows, NSA_DK), BF16),
            jax.ShapeDtypeStruct((rows // 256, 128, 256), BF16),
            jax.ShapeDtypeStruct((NSA_GROUPS, rows, NSA_DK), BF16),
            jax.ShapeDtypeStruct((rows // 128, 128, 128), BF16),
            jax.ShapeDtypeStruct((rows // MLA_KC, KV_LORA, MLA_KC), BF16),
        ]
        out_specs += [
            pl.BlockSpec((NSA_GROUPS, tm, NSA_DK), lambda i: (0, i, 0)),
            pl.BlockSpec((tm // 256, 128, 256), lambda i: (i, 0, 0)),
            pl.BlockSpec((NSA_GROUPS, tm, NSA_DK), lambda i: (0, i, 0)),
            pl.BlockSpec((tm // 128, 128, 128), lambda i: (i, 0, 0)),
            pl.BlockSpec((tm // MLA_KC, KV_LORA, MLA_KC), lambda i: (i, 0, 0)),
        ]
    return pl.pallas_call(
        functools.partial(_proj_body, prompt),
        grid=(rows // tm,),
        in_specs=[row_spec(D_MODEL), _full(wp.shape), _full(wuq.shape), _full(wuk.shape),
                  _full(wqr.shape), _full(gq.shape), _full(gkv.shape),
                  row_spec(LANES), row_spec(LANES), row_spec(LANES)],
        out_specs=out_specs,
        out_shape=out_shapes,
        compiler_params=_cparams(1),
        name="proj",
    )(x, wp, wuq, wuk, wqr, gq, gkv, cos, s_lo, s_hi)


def _rope_tables(pos):
    half = D_ROPE // 2
    freq = ROPE_THETA ** (-jnp.arange(half, dtype=F32) / half)
    ang = pos.astype(F32)[:, None] * freq
    cos, sin = jnp.cos(ang), jnp.sin(ang)
    z = jnp.zeros((pos.shape[0], LANES - D_ROPE), F32)
    zh = jnp.zeros_like(sin)
    return (jnp.concatenate([cos, cos, z], 1), jnp.concatenate([-sin, zh, z], 1),
            jnp.concatenate([zh, sin, z], 1))


def _proj_weights(w_in, g_q, w_uq, w_qr, g_kv, w_uk):
    offs = np.cumsum((0,) + IN_WIDTHS)
    col = lambda i: w_in[:, offs[i]:offs[i + 1]]
    pad = lambda a, w: jnp.pad(a, ((0, 0), (0, w - a.shape[1])))
    ng = col(2).reshape(D_MODEL, NSA_HEADS, 3).transpose(0, 2, 1).reshape(D_MODEL, 24)
    wp = jnp.concatenate([col(0), col(1), pad(ng, 128), col(3), col(4), pad(col(5), 128), col(6)], 1)
    wuq = w_uq.reshape(Q_LORA, MLA_HEADS * D_NOPE)
    eye = jnp.eye(MLA_HEADS, dtype=F32)
    wuk = jnp.einsum('rhd,hg->hdgr', w_uk, eye).reshape(MLA_HEADS * D_NOPE, MLA_HEADS * KV_LORA)
    wqr = jnp.pad(w_qr, ((0, 0), (0, 0), (0, LANES - D_ROPE))).reshape(Q_LORA, MLA_HEADS * LANES)
    return (wp.astype(BF16), wuq.astype(BF16), wuk.astype(BF16), wqr.astype(BF16),
            g_q.reshape(1, Q_LORA), g_kv.reshape(1, KV_LORA))


def _compress_weights(cmp_pos, cmp_w1, cmp_w2):
    eye = jnp.eye(NSA_GROUPS, dtype=F32)
    w1r = cmp_w1.reshape(2, CMP_BLOCK, NSA_DK, CMP_HIDDEN)
    w1 = jnp.einsum('ktdc,gG->ktgdGc', w1r, eye).reshape(2, CMP_BLOCK, 128, 256)
    w2 = jnp.einsum('kcd,gG->kgcGd', cmp_w2, eye).reshape(2, 256, 128)
    pos = jnp.concatenate([cmp_pos, cmp_pos], axis=-1)
    return pos, w1.astype(BF16), w2.astype(BF16)


def _compress_rows(x_ref, kv, nblk, pos_ref, w1_ref, w2_ref, transposed=False):
    acc = jnp.zeros((nblk, 256), F32)
    for t in range(CMP_BLOCK):
        xt = x_ref[pl.ds(t, nblk, stride=CMP_BLOCK), :]
        acc = acc + _dot((xt + pos_ref[kv, t:t + 1, :]).astype(BF16), w1_ref[kv, t])
    h = (acc * jax.nn.sigmoid(acc)).astype(BF16)
    if transposed:
        return _dot_nt(w2_ref[kv], h)
    return _dot(h, w2_ref[kv])


def _compress_prompt_body(xk_ref, xv_ref, pos_ref, w1_ref, w2_ref, w2t_ref, kc_ref, vct_ref):
    nblk = xk_ref.shape[0] // CMP_BLOCK
    kcb = _compress_rows(xk_ref, 0, nblk, pos_ref, w1_ref, w2_ref).astype(BF16)
    vt = _compress_rows(xv_ref, 1, nblk, pos_ref, w1_ref, w2t_ref, transposed=True).astype(BF16)
    for g in range(NSA_GROUPS):
        kc_ref[g] = kcb[:, g * NSA_DK:(g + 1) * NSA_DK]
        vct_ref[g] = vt[g * NSA_DK:(g + 1) * NSA_DK, :]


def _compress_prompt(nkv4, cw):
    t = nkv4.shape[0]
    rows = min(t, 4096)
    nblk, nc = rows // CMP_BLOCK, t // CMP_BLOCK
    pos, w1, w2 = cw
    w2t = w2.transpose(0, 2, 1)
    return pl.pallas_call(
        _compress_prompt_body,
        grid=(t // rows,),
        in_specs=[pl.BlockSpec((rows, 128), lambda i: (i, 0)), pl.BlockSpec((rows, 128), lambda i: (i, 1)),
                  _full(pos.shape), _full(w1.shape), _full(w2.shape), _full(w2t.shape)],
        out_specs=[pl.BlockSpec((NSA_GROUPS, nblk, NSA_DK), lambda i: (0, i, 0)),
                   pl.BlockSpec((NSA_GROUPS, NSA_DK, nblk), lambda i: (0, 0, i))],
        out_shape=[jax.ShapeDtypeStruct((NSA_GROUPS, nc, NSA_DK), BF16),
                   jax.ShapeDtypeStruct((NSA_GROUPS, NSA_DK, nc), BF16)],
        compiler_params=_cparams(1),
        name="compress_prompt",
    )(nkv4, nkv4, pos, w1, w2, w2t)


def _t5_bucket(dist):
    n = jnp.maximum(dist, 0)
    max_exact = N_BUCKETS // 2
    nf = jnp.maximum(n, 1).astype(F32)
    large = max_exact + (jnp.log(nf / max_exact) / math.log(MAX_DISTANCE / max_exact)
                         * (N_BUCKETS - max_exact)).astype(jnp.int32)
    large = jnp.minimum(large, N_BUCKETS - 1)
    return jnp.where(n < max_exact, n, large)


FAR_DIST = 129


def _rel_bias(rel_bias, dist):
    onehot = jax.nn.one_hot(_t5_bucket(dist), N_BUCKETS, dtype=F32)
    b = jnp.einsum('...b,bh->...h', onehot, rel_bias - rel_bias[N_BUCKETS - 1],
                   precision=lax.Precision.HIGHEST)
    b = jnp.where((dist >= 0)[..., None], b, NEG)
    return jnp.moveaxis(b, -1, 0)


def _lanes_hq(b):
    k = b.shape[1]
    return b.reshape(NSA_GROUPS, NSA_HPG, k, 128).transpose(0, 2, 1, 3).reshape(NSA_GROUPS, k, 512)


def _prompt_bias_tables(rel_bias):
    q = jnp.arange(128)[None, :]
    k = jnp.arange(128)[:, None]
    zero = jnp.zeros((NSA_GROUPS, 128, 512), F32)
    sub = _lanes_hq(_rel_bias(rel_bias, 128 + q - k))
    diag = _lanes_hq(_rel_bias(rel_bias, q - k))
    neg = jnp.full((NSA_GROUPS, 128, 512), NEG, F32)
    anti = _lanes_hq(jnp.broadcast_to(jnp.where(k >= q, 0.0, NEG)[None], (8, 128, 128)))
    near = jnp.stack([zero, sub, diag, neg, anti], axis=1)
    r = jnp.arange(16)[:, None]
    cmpw = jnp.stack([_lanes_hq(_rel_bias(rel_bias, q - CMP_BLOCK * (r - off) - (CMP_BLOCK - 1)))
                      for off in (4, 8, 0)], axis=1)
    return near, cmpw


T_ZERO, T_SUB, T_DIAG, T_NEG, T_ANTI = range(5)


def _softmax_update(s, vt, m_old, l_old, acc_old, scale=None):
    m_new = jnp.maximum(m_old, jnp.max(s, axis=0, keepdims=True))
    if scale is None:
        alpha = jnp.exp(m_old - m_new)
        e = jnp.exp(s - m_new)
    else:
        c = scale * math.log2(math.e)
        alpha = jnp.exp2((m_old - m_new) * c)
        e = jnp.exp2((s - m_new) * c)
    d = vt.shape[0]
    pv = _dot(jnp.concatenate([vt, jnp.ones((BF16_ROWS, vt.shape[1]), BF16)], axis=0), e.astype(BF16))
    return m_new, alpha * l_old + pv[d:d + 1], alpha * acc_old + pv[:d]


def _pipelined(n_groups, chunks, qk, finish, carry, rows, scale=None):
    carry = list(carry)
    raws = [qk(g, chunks[0][0], len(chunks)) for g in range(n_groups)]
    for u, chunk in enumerate(chunks):
        for g in range(n_groups):
            raw = raws[g][u * rows:(u + 1) * rows]
            carry[g] = _softmax_update(*finish(g, chunk, raw), *carry[g], scale=scale)
    return tuple(carry)


def _topk_rows(score, n_sel):
    nb = score.shape[0]
    blk = lax.broadcasted_iota(jnp.int32, score.shape, 0)
    sel = jnp.zeros(score.shape, F32)
    picks = []
    for _ in range(n_sel):
        m = jnp.max(score, axis=0, keepdims=True)
        j = jnp.min(jnp.where(score == m, blk, nb), axis=0, keepdims=True)
        hit = blk == j
        sel = jnp.where(hit, 1.0, sel)
        score = jnp.where(hit, -2.0, score)
        picks.append(j)
    return sel, picks


def _nsa_prompt_body(q_ref, gate_ref, kc_ref, vct_ref, kslc_ref, vslct_ref, kwin_ref, vwint_ref,
                     near_ref, cmpw_ref, o_ref, s_ref, imp_ref, sel_ref, m_ref, l_ref, acc_ref, out_ref):
    i = pl.program_id(0)
    nc = kc_ref.shape[1]
    nb = nc // 2
    gate_t = gate_ref[...].T
    lane_q = lax.broadcasted_iota(jnp.int32, (1, 128), 1)
    cur = 2 * i + (lane_q >= SEL_BLOCK).astype(jnp.int32)
    odd = i % 2
    w0 = pl.multiple_of(jnp.where(i == 0, 0, jnp.where(odd == 1, 4 * i - 4, 4 * i - 8)), 8)
    var = jnp.where(i == 0, 2, jnp.where(odd == 1, 0, 1))
    jl = i // 2
    even = 1 - odd

    def reset():
        m_ref[...] = jnp.full(m_ref.shape, M_INIT, F32)
        l_ref[...] = jnp.zeros(l_ref.shape, F32)
        acc_ref[...] = jnp.zeros(acc_ref.shape, F32)

    def gate_row(g, b):
        return jnp.concatenate([gate_t[b * 8 + 4 * g + h:b * 8 + 4 * g + h + 1, :]
                                for h in range(NSA_HPG)], axis=1)

    groups = range(NSA_GROUPS)

    def q_group(g):
        return q_ref[4 * g:4 * g + 4].reshape(4 * 128, NSA_DK)

    for g in groups:
        s_ref[g, pl.ds(nc, 16), :] = jnp.zeros((16, 512), F32)
        s_ref[g, pl.ds(0, nc), :] = _dot_nt(kc_ref[g], q_group(g))
        s_ref[g, pl.ds(w0, 16), :] = s_ref[g, pl.ds(w0, 16), :] + cmpw_ref[g, var]
        row_n = lax.broadcasted_iota(jnp.int32, (nc, 1), 0)
        s = jnp.where(row_n <= 4 * i + 3, s_ref[g, pl.ds(0, nc), :], NEG)
        m = jnp.maximum(jnp.max(s, axis=0, keepdims=True), M_INIT)
        e = jnp.exp(s - m)
        d = jnp.sum(e, axis=0, keepdims=True)
        p = e / jnp.where(d > 0.0, d, 1.0)
        out_ref[g] = gate_row(g, 0) * _dot(vct_ref[g], p.astype(BF16))
        imp_ref[g] = p[:, 0:128] + p[:, 128:256] + p[:, 256:384] + p[:, 384:512]
        imp = imp_ref[g, pl.ds(0, nb, stride=2), :] + imp_ref[g, pl.ds(1, nb, stride=2), :]
        blk = lax.broadcasted_iota(jnp.int32, (nb, 128), 0)
        valid = blk <= cur
        forced = valid & ((blk == 0) | (blk == cur) | (blk == cur - 1))
        score = jnp.where(forced, NSA_HPG + 1.0, jnp.where(valid, imp, -1.0))
        sel_ref[g], _ = _topk_rows(score, min(N_SEL, nb))

    def sel_qk(g, j0, n):
        rows = kslc_ref[g, pl.ds(pl.multiple_of(j0 * 256, 256), n * 256), :]
        return _dot_nt(rows, q_group(g))

    def sel_scores(g, chunk, s):
        j, top, bot = chunk
        parts = []
        for b in range(4):
            row = sel_ref[g, pl.ds(4 * j + b, 1), :]
            mask = jnp.concatenate([row] * NSA_HPG, axis=1) > 0.5
            sb = s[b * SEL_BLOCK:(b + 1) * SEL_BLOCK]
            if top is not None:
                tab = near_ref[g, top if b < 2 else bot]
                sb = sb + tab[(b % 2) * SEL_BLOCK:(b % 2 + 1) * SEL_BLOCK]
            parts.append(jnp.where(mask, sb, NEG))
        return jnp.concatenate(parts, axis=0), vslct_ref[j, g * NSA_DK:(g + 1) * NSA_DK, :]

    def load_state():
        return tuple((m_ref[g], l_ref[g], acc_ref[g]) for g in groups)

    def store_state(st):
        for g in groups:
            m_ref[g], l_ref[g], acc_ref[g] = st[g]

    sel_run = functools.partial(_pipelined, NSA_GROUPS, qk=sel_qk, finish=sel_scores, rows=256)
    n_far = jnp.maximum(jl - 1, 0)
    unroll = 4
    init = tuple((jnp.full((1, 512), M_INIT, F32), jnp.zeros((1, 512), F32),
                  jnp.zeros((NSA_DK, 512), F32)) for _ in groups)
    far = lax.fori_loop(
        0, n_far // unroll,
        lambda jj, c: sel_run([(unroll * jj + u, None, None) for u in range(unroll)], carry=c), init)
    far = lax.fori_loop(n_far - n_far % unroll, n_far,
                        lambda j, c: sel_run([(j, None, None)], carry=c), far)
    store_state(far)
    last_tabs = (jl, jnp.where(even == 1, T_DIAG, T_SUB), jnp.where(even == 1, T_NEG, T_DIAG))

    @pl.when(jl >= 1)
    def _():
        prev_tabs = (jl - 1, T_ZERO, jnp.where(even == 1, T_SUB, T_ZERO))
        store_state(sel_run([prev_tabs, last_tabs], carry=load_state()))

    @pl.when(jl < 1)
    def _():
        store_state(sel_run([last_tabs], carry=load_state()))

    for g in groups:
        out_ref[g] = out_ref[g] + (gate_row(g, 1) / l_ref[g]) * acc_ref[g]

    def win_qk(g, j0, n):
        return _dot_nt(kwin_ref[g, pl.ds(pl.multiple_of(j0 * 128, 128), n * 128), :], q_group(g))

    def win_finish(g, chunk, s):
        j, tab = chunk
        if tab is not None:
            s = s + near_ref[g, tab]
        return s, vwint_ref[j, g * NSA_DK:(g + 1) * NSA_DK, :]

    win_run = functools.partial(_pipelined, NSA_GROUPS, qk=win_qk, finish=win_finish, rows=128)
    win_chunks = ((4, T_ANTI), (3, None), (2, None), (1, T_SUB), (0, T_DIAG))
    reset()

    @pl.when(i >= 4)
    def _():
        store_state(win_run([(i - back, tab) for back, tab in win_chunks], carry=load_state()))

    @pl.when(i < 4)
    def _():
        for back, tab in win_chunks:
            @pl.when(i >= back)
            def _(back=back, tab=tab):
                store_state(win_run([(i - back, tab)], carry=load_state()))

    for g in groups:
        out = out_ref[g] + (gate_row(g, 2) / l_ref[g]) * acc_ref[g]
        for h in range(NSA_HPG):
            col = (4 * g + h) * NSA_DK
            o_ref[:, col:col + NSA_DK] = out[:, h * 128:(h + 1) * 128].T.astype(BF16)


def _nsa_prompt(qn, gate, kc, vct, kslc, vslct, kwin, vwint, near, cmpw):
    t = qn.shape[1]
    nc = kc.shape[1]
    return pl.pallas_call(
        _nsa_prompt_body,
        grid=(t // 128,),
        in_specs=[pl.BlockSpec((NSA_HEADS, 128, NSA_DK), lambda i: (0, i, 0)),
                  pl.BlockSpec((128, LANES), lambda i: (i, 0)),
                  _full(kc.shape), _full(vct.shape), _full(kslc.shape), _full(vslct.shape),
                  _full(kwin.shape), _full(vwint.shape), _full(near.shape), _full(cmpw.shape)],
        out_specs=pl.BlockSpec((128, 512), lambda i: (i, 0)),
        out_shape=jax.ShapeDtypeStruct((t, 512), BF16),
        scratch_shapes=[pltpu.VMEM((NSA_GROUPS, nc + 16, 512), F32), pltpu.VMEM((NSA_GROUPS, nc, 128), F32),
                        pltpu.VMEM((NSA_GROUPS, nc // 2, 128), F32), pltpu.VMEM((NSA_GROUPS, 1, 512), F32),
                        pltpu.VMEM((NSA_GROUPS, 1, 512), F32), pltpu.VMEM((NSA_GROUPS, NSA_DK, 512), F32),
                        pltpu.VMEM((NSA_GROUPS, NSA_DK, 512), F32)],
        compiler_params=_cparams(1),
        name="nsa_prompt",
    )(qn, gate, kc, vct, kslc, vslct, kwin, vwint, near, cmpw)


MLA_TQ = 128
MLA_KC = 256
MLA_HALF_HEADS = 4
MLA_SCALE = (D_NOPE + D_ROPE) ** -0.5


def _mla_mask_tables():
    k = np.arange(128)[:, None]
    q = np.arange(128)[None, :]
    diag = np.tile(np.where(k <= q, 0.0, NEG), (1, MLA_HALF_HEADS))
    return jnp.asarray(np.stack([np.zeros_like(diag), diag, np.full_like(diag, NEG)]), F32)


M_ZERO, M_DIAG, M_NEG = range(3)


def _mla_prompt_body(q_ref, k_ref, vt_ref, tab_ref, o_ref):
    i = pl.program_id(0)
    halves = range(MLA_HEADS // MLA_HALF_HEADS)
    jl = i // 2
    even = 1 - i % 2

    def q_half(hh):
        return jnp.concatenate([q_ref[:, (hh * MLA_HALF_HEADS + h) * 256:(hh * MLA_HALF_HEADS + h + 1) * 256]
                                for h in range(MLA_HALF_HEADS)], axis=0)

    def qk(hh, j0, n):
        return _dot_nt(k_ref[pl.ds(pl.multiple_of(j0 * MLA_KC, MLA_KC), n * MLA_KC), :], q_half(hh))

    def finish(hh, chunk, s):
        j, top, bot = chunk
        if top is not None:
            s = s + jnp.concatenate([tab_ref[top], tab_ref[bot]], axis=0)
        return s, vt_ref[j]

    run = functools.partial(_pipelined, len(halves), qk=qk, finish=finish, rows=MLA_KC, scale=MLA_SCALE)
    lanes = MLA_HALF_HEADS * MLA_TQ
    init = tuple((jnp.full((1, lanes), M_INIT, F32), jnp.zeros((1, lanes), F32),
                  jnp.zeros((KV_LORA, lanes), F32)) for _ in halves)
    unroll = 4
    st = lax.fori_loop(0, jl // unroll,
                       lambda jj, c: run([(unroll * jj + u, None, None) for u in range(unroll)], carry=c), init)
    st = lax.fori_loop(jl - jl % unroll, jl, lambda j, c: run([(j, None, None)], carry=c), st)
    st = run([(jl, jnp.where(even == 1, M_DIAG, M_ZERO), jnp.where(even == 1, M_NEG, M_DIAG))], carry=st)
    for hh in halves:
        _, l, acc = st[hh]
        o = acc / l
        for h in range(MLA_HALF_HEADS):
            head = hh * MLA_HALF_HEADS + h
            o_ref[:, head * KV_LORA:(head + 1) * KV_LORA] = o[:, h * MLA_TQ:(h + 1) * MLA_TQ].T.astype(BF16)


def _mla_prompt(qmla, kmla, ckvt):
    t = qmla.shape[0]
    tabs = _mla_mask_tables()
    return pl.pallas_call(
        _mla_prompt_body,
        grid=(t // MLA_TQ,),
        in_specs=[pl.BlockSpec((MLA_TQ, 2048), lambda i: (i, 0)), _full(kmla.shape), _full(ckvt.shape),
                  _full(tabs.shape)],
        out_specs=pl.BlockSpec((MLA_TQ, MLA_HEADS * KV_LORA), lambda i: (i, 0)),
        out_shape=jax.ShapeDtypeStruct((t, MLA_HEADS * KV_LORA), BF16),
        compiler_params=_cparams(1),
        name="mla_prompt",
    )(qmla, kmla, ckvt, tabs)


def _mem_kv_body(x_ref, w_ref, o_ref, ob_ref):
    kv = _dot(x_ref[...].astype(BF16), w_ref[...])
    o_ref[...] = kv
    ob_ref[...] = kv.astype(BF16)


def _mem_kv(mem, w):
    shp = (MEM_TOKENS, 2 * MEM_HEADS * MEM_DH)
    return pl.pallas_call(
        _mem_kv_body,
        in_specs=[_full(mem.shape), _full(w.shape)],
        out_specs=[_full(shp), _full(shp)],
        out_shape=[jax.ShapeDtypeStruct(shp, F32), jax.ShapeDtypeStruct(shp, BF16)],
        grid=(1,),
        compiler_params=_cparams(1),
        name="mem_kv",
    )(mem, w)


def _mem_attn_body(q_ref, kv_ref, o_ref):
    for h in range(MEM_HEADS):
        sl = slice(h * MEM_DH, (h + 1) * MEM_DH)
        k = kv_ref[:, sl]
        v = kv_ref[:, MEM_HEADS * MEM_DH + h * MEM_DH:MEM_HEADS * MEM_DH + (h + 1) * MEM_DH]
        s = _dot_nt(q_ref[:, sl], k) * (MEM_DH ** -0.5)
        e = jnp.exp(s - jnp.max(s, axis=1, keepdims=True))
        p = e / jnp.sum(e, axis=1, keepdims=True)
        o_ref[:, sl] = _dot(p.astype(BF16), v).astype(BF16)


def _mem_attn(mq, kvb, tm):
    t = mq.shape[0]
    return pl.pallas_call(
        _mem_attn_body,
        grid=(t // tm,),
        in_specs=[pl.BlockSpec((tm, 512), lambda i: (i, 0)), _full(kvb.shape)],
        out_specs=pl.BlockSpec((tm, 512), lambda i: (i, 0)),
        out_shape=jax.ShapeDtypeStruct((t, 512), BF16),
        compiler_params=_cparams(1),
        name="mem_attn",
    )(mq, kvb)


def _merge_weights(w_in, w_uv, w_br, w_o):
    wmg = w_in[:, sum(IN_WIDTHS[:-1]):]
    eye = jnp.eye(MLA_HEADS, dtype=F32)
    wuv = jnp.einsum('rhd,hg->hrgd', w_uv, eye).reshape(MLA_HEADS * KV_LORA, MLA_HEADS * D_V)
    return wmg.astype(BF16), wuv.astype(BF16), w_br.astype(BF16), w_o.astype(BF16)


def _merge_body(x_ref, onsa_ref, olat_ref, omem_ref, wmg_ref, wuv_ref, wbr_ref, wo_ref, g_ref, b_ref,
                o_ref):
    x = x_ref[...]
    xb = x.astype(BF16)
    v_mla = _dot(olat_ref[...], wuv_ref[...]).astype(BF16)
    tot = jnp.zeros(x.shape, F32)
    for b, br in enumerate((onsa_ref[...], v_mla, omem_ref[...])):
        gate = jax.nn.sigmoid(_dot(xb, wmg_ref[:, b * D_MODEL:(b + 1) * D_MODEL]))
        tot = tot + gate * _dot(br, wbr_ref[b])
    mix = _dot(tot.astype(BF16), wo_ref[...])
    o_ref[...] = _layer_norm(ALPHA * x + mix, g_ref[...], b_ref[...])


def _merge(x, onsa, olat, omem, mw, g, b, tm):
    rows = x.shape[0]
    wmg, wuv, wbr, wo = mw
    row_spec = lambda w: pl.BlockSpec((tm, w), lambda i: (i, 0))
    return pl.pallas_call(
        _merge_body,
        grid=(rows // tm,),
        in_specs=[row_spec(D_MODEL), row_spec(512), row_spec(1024), row_spec(512),
                  _full(wmg.shape), _full(wuv.shape), _full(wbr.shape), _full(wo.shape),
                  _full(g.shape), _full(b.shape)],
        out_specs=row_spec(D_MODEL),
        out_shape=jax.ShapeDtypeStruct((rows, D_MODEL), F32),
        compiler_params=_cparams(1),
        name="merge",
    )(x, onsa, olat, omem, mw[0], wuv, wbr, wo, g, b)


def _prompt_path(x_prompt, mem_prompt, rel_bias, ln_g, ln_b, ffn_w, pw, cw, mw, mem_w_kv):
    t = x_prompt.shape[1]
    tm = 512
    ln = lambda k: (ln_g[0, k][None], ln_b[0, k][None])
    x1 = _ffn(x_prompt[0], *ffn_w[0], *ln(0), tm)
    (qn, nkv4, win, gate, qmla, row, kmla, mq, kslc, vslct, kwin, vwint, ckvt) = _proj(
        x1, pw, _rope_tables(jnp.arange(t)), tm, True)
    kc, vct = _compress_prompt(nkv4, cw)
    near, cmpw = _prompt_bias_tables(rel_bias)
    o_nsa = _nsa_prompt(qn, gate, kc, vct, kslc, vslct, kwin, vwint, near, cmpw)
    o_lat = _mla_prompt(qmla, kmla, ckvt)
    mem_kv, mem_kvb = _mem_kv(mem_prompt[0], mem_w_kv[0].reshape(D_MODEL, -1).astype(BF16))
    o_mem = _mem_attn(mq, mem_kvb, tm)
    x2 = _merge(x1, o_nsa, o_lat, o_mem, mw, *ln(1), tm)
    y = _ffn(x2, *ffn_w[1], *ln(2), tm)
    wb = min(WINDOW, t)
    return (y[None],
            nkv4.reshape(1, 1, t, 4, NSA_GROUPS, NSA_DK),
            row.reshape(1, 1, t, MLA_ROW),
            win[t - wb:].reshape(1, 1, wb, 2, NSA_GROUPS, NSA_DK),
            mem_kv.reshape(1, 1, MEM_TOKENS, 2, MEM_HEADS, MEM_DH))


def _own_group_lanes(q):
    q2 = jnp.concatenate([q, q], axis=1)
    head = lax.broadcasted_iota(jnp.int32, q2.shape, 0)
    lane = lax.broadcasted_iota(jnp.int32, q2.shape, 1)
    return jnp.where(head // NSA_HPG == lane // NSA_DK, q2, jnp.zeros_like(q2))


def _bf(x):
    return x.astype(BF16)


def _new_key_score(qm, k_new):
    return jnp.sum(qm.astype(F32) * _bf(k_new).astype(F32), axis=1, keepdims=True)


CMP_PITCH = 136


def _sample_compress_weights(cmp_pos, cmp_w1, cmp_w2):
    eye4 = jnp.eye(PAGE // CMP_BLOCK, dtype=F32)
    w1p = cmp_w1.reshape(2, CMP_BLOCK, NSA_DK // 2, 2, CMP_HIDDEN)
    w1 = jnp.einsum('ktpjc,nm->kpjntmc', w1p, eye4).reshape(2, NSA_DK // 2, 256, 512)
    eye2 = jnp.eye(NSA_GROUPS, dtype=F32)
    w2 = jnp.einsum('kcd,gG->kgcGd', cmp_w2, eye2).reshape(2, 256, 128)
    pos = jnp.tile(cmp_pos.transpose(0, 2, 1), (1, 1, PAGE // CMP_BLOCK))
    return pos, w1.astype(BF16), w2.astype(BF16)


def _s_compress_body(pt_ref, cache_ref, pos_ref, w1_ref, w2_ref, out_ref, buf, sem):
    kv, b = pl.program_id(0), pl.program_id(1)
    db = pl.num_programs(1)
    n_pages = pt_ref.shape[1]
    n = kv * db + b
    slot = n % 2

    def copies(step, sl, wait):
        skv, sb = step // db, step % db

        for p in range(n_pages):
            cp = pltpu.make_async_copy(
                cache_ref.at[pt_ref[sb, p], pl.ds(pl.multiple_of(skv * 128, 128), 128), :],
                buf.at[sl, pl.ds(p * CMP_PITCH, 128), :], sem.at[sl])
            cp.wait() if wait else cp.start()

    @pl.when(n == 0)
    def _():
        copies(0, 0, False)

    def compress():
        page_rows = buf.at[slot]
        acc = jnp.zeros((NSA_GROUPS * n_pages, 512), F32)
        for dp in range(NSA_DK // 2):
            rows = []
            for g in range(NSA_GROUPS):
                pair = [page_rows[pl.ds(g * NSA_DK + 2 * dp + j, n_pages, stride=CMP_PITCH), :]
                        + pos_ref[0, 2 * dp + j:2 * dp + j + 1, :] for j in range(2)]
                rows.append(jnp.concatenate(pair, axis=1))
            acc = acc + _dot(_bf(jnp.concatenate(rows, axis=0)), w1_ref[0, dp])
        h = _bf(acc * jax.nn.sigmoid(acc))
        for nb in range(PAGE // CMP_BLOCK):
            hh = jnp.concatenate([h[g * n_pages:(g + 1) * n_pages, nb * 128:(nb + 1) * 128]
                                  for g in range(NSA_GROUPS)], axis=1)
            out_ref[0, 0, nb] = _bf(_dot(hh, w2_ref[0]))

    @pl.when(n + 1 < 2 * db)
    def _():
        copies(n, slot, True)
        copies(n + 1, 1 - slot, False)
        compress()

    @pl.when(n + 1 >= 2 * db)
    def _():
        copies(n, slot, True)
        compress()


def _s_compress(page_table, cache_t, cw):
    db, n_pages = page_table.shape
    pos, w1, w2 = cw
    kvspec = lambda shape: pl.BlockSpec((1,) + shape[1:], lambda kv, b, pt: (kv,) + (0,) * (len(shape) - 1))
    nbk = PAGE // CMP_BLOCK
    return pl.pallas_call(
        _s_compress_body,
        grid_spec=pltpu.PrefetchScalarGridSpec(
            num_scalar_prefetch=1,
            grid=(2, db),
            in_specs=[pl.BlockSpec(memory_space=pl.ANY), kvspec(pos.shape), kvspec(w1.shape), kvspec(w2.shape)],
            out_specs=pl.BlockSpec((1, 1, nbk, n_pages, 128), lambda kv, b, pt: (kv, b, 0, 0, 0)),
            scratch_shapes=[pltpu.VMEM((2, n_pages * CMP_PITCH, 128), F32), pltpu.SemaphoreType.DMA((2,))],
        ),
        out_shape=jax.ShapeDtypeStruct((2, db, nbk, n_pages, 128), BF16),
        compiler_params=_cparams(2),
        name="sample_compress",
    )(page_table, cache_t, pos, w1, w2)


def _s_cmp_win_body(cmp_ref, q_ref, g_ref, st_ref, wnew_ref, bc_ref, bw_ref, bn_ref, o8_ref, imp_ref):
    nc = bc_ref.shape[1]
    kc = cmp_ref[0, 0].reshape(nc, 128)
    vc = cmp_ref[1, 0].reshape(nc, 128)
    qm = _own_group_lanes(q_ref[0])
    gate = g_ref[0]
    s = _dot_nt(qm, kc) + bc_ref[...]
    e = jnp.exp(s - jnp.max(s, axis=1, keepdims=True))
    p = e / jnp.sum(e, axis=1, keepdims=True)
    out = gate[:, 0:1] * _dot(_bf(p), vc)
    for g in range(NSA_GROUPS):
        imp_ref[0, g:g + 1, :] = jnp.sum(p[NSA_HPG * g:NSA_HPG * (g + 1)], axis=0, keepdims=True)
    st = st_ref[0]
    wnew = wnew_ref[0]
    s = _dot(qm, _bf(st[:128])) + bw_ref[...]
    s_new = _new_key_score(qm, wnew[:, :128]) + bn_ref[...]
    m = jnp.maximum(jnp.max(s, axis=1, keepdims=True), s_new)
    e = jnp.exp(s - m)
    e_new = jnp.exp(s_new - m)
    l = jnp.sum(e, axis=1, keepdims=True) + e_new
    acc = _dot_nt(_bf(e), _bf(st[128:])) + _bf(e_new).astype(F32) * _bf(wnew[:, 128:]).astype(F32)
    o8_ref[0] = out + (gate[:, 2:3] / l) * acc


def _s_cmp_win(cmp, q3, g3, state_t, wnew, bc, bw, bn):
    db = q3.shape[0]
    nc = bc.shape[1]
    wb = state_t.shape[2]
    one = lambda *tail: pl.BlockSpec((1,) + tail, lambda b: (b,) + (0,) * len(tail))
    return pl.pallas_call(
        _s_cmp_win_body,
        grid=(db,),
        in_specs=[pl.BlockSpec((2, 1) + cmp.shape[2:], lambda b: (0, b, 0, 0, 0)),
                  one(NSA_HEADS, NSA_DK), one(NSA_HEADS, 3), one(256, wb), one(1, 256),
                  _full(bc.shape), _full(bw.shape), _full(bn.shape)],
        out_specs=[one(NSA_HEADS, 128), one(NSA_GROUPS, nc)],
        out_shape=[jax.ShapeDtypeStruct((db, NSA_HEADS, 128), F32),
                   jax.ShapeDtypeStruct((db, NSA_GROUPS, nc), F32)],
        compiler_params=_cparams(1),
        name="sample_cmp_win",
    )(cmp, q3, g3, state_t, wnew, bc, bw, bn)


def _s_topk_body(n_blk, imp_ref, idx_ref, t_ref):
    db, nc = imp_ref.shape[0], imp_ref.shape[2]
    nbp = nc // 2
    rows = t_ref.shape[0] // 2
    cur = n_blk - 1
    for g in range(NSA_GROUPS):
        t_ref[pl.ds(0, nc), :] = imp_ref[:, g, :].T
        t_ref[pl.ds(nc, t_ref.shape[0] - nc), :] = jnp.zeros((t_ref.shape[0] - nc, db), F32)
        imp = t_ref[pl.ds(0, rows, stride=2), :] + t_ref[pl.ds(1, rows, stride=2), :]
        blk = lax.broadcasted_iota(jnp.int32, (rows, db), 0)
        forced = (blk == 0) | (blk == cur) | (blk == cur - 1)
        score = jnp.where(blk >= n_blk, -2.0, jnp.where(forced, NSA_HPG + 1.0, imp))
        _, picks = _topk_rows(score, min(N_SEL, n_blk))
        idx_ref[g] = jnp.concatenate(picks, axis=0)


def _s_topk(imp, n_blk):
    db, _, nc = imp.shape
    rows = -(-n_blk // 8) * 8
    n_sel = min(N_SEL, n_blk)
    return pl.pallas_call(
        functools.partial(_s_topk_body, n_blk),
        grid=(1,),
        in_specs=[_full(imp.shape)],
        out_specs=_full((NSA_GROUPS, n_sel, db)),
        out_shape=jax.ShapeDtypeStruct((NSA_GROUPS, n_sel, db), jnp.int32),
        scratch_shapes=[pltpu.VMEM((2 * rows, db), F32)],
        compiler_params=_cparams(1),
        name="sample_topk",
    )(imp)


def _s_select_body(n_sel, pt_ref, idx_ref, cache_ref, q_ref, g_ref, new_ref, o8in_ref, ta_ref, tb_ref,
                   bn_ref, o8_ref, buf, sem):
    b = pl.program_id(0)
    nbp = pt_ref.shape[1] * (PAGE // SEL_BLOCK)
    slot = b % 2
    n_dma = NSA_GROUPS * n_sel

    def copy(seq, sl, n):
        blk = jnp.minimum(idx_ref[n, seq], nbp - 1)
        return pltpu.make_async_copy(cache_ref.at[pt_ref[seq, blk // 2], pl.ds(256, 256), :],
                                     buf.at[sl, n // n_sel, :, pl.ds((n % n_sel) * PAGE, PAGE)], sem.at[sl])

    def start(seq, sl):
        for n in range(n_dma):
            copy(seq, sl, n).start()

    @pl.when(b == 0)
    def _():
        start(0, 0)

    @pl.when(b + 1 < pl.num_programs(0))
    def _():
        start(b + 1, 1 - slot)

    for n in range(n_dma):
        copy(b, slot, n).wait()
    qm = _own_group_lanes(q_ref[0])
    gate = g_ref[0]
    new = new_ref[0]
    head = lax.broadcasted_iota(jnp.int32, (NSA_HEADS, 1), 0)
    upper = (lax.broadcasted_iota(jnp.int32, (1, PAGE), 1) >= SEL_BLOCK).astype(F32)
    s_new = _new_key_score(qm, new[:, 256:384]) + bn_ref[...]
    v_new = _bf(new[:, 384:512]).astype(F32)
    out = o8in_ref[0]
    for g in range(NSA_GROUPS):
        pieces = []
        for r in range(n_sel):
            blk = idx_ref[g * n_sel + r, b]
            wa = jnp.where(blk == nbp - 1, 1.0, 0.0)
            wb = jnp.where(blk == nbp - 2, 1.0, 0.0)
            wm = jnp.where(blk >= nbp, NEG, 0.0)
            hb = (jnp.minimum(blk, nbp - 1) % 2).astype(F32)
            other_half = upper + hb - 2.0 * upper * hb
            pieces.append(wa * ta_ref[...] + wb * tb_ref[...] + wm + other_half * NEG)
        s = _dot(qm, _bf(buf[slot, g, pl.ds(0, 128), :])) + jnp.concatenate(pieces, axis=1)
        m = jnp.maximum(jnp.max(s, axis=1, keepdims=True), s_new)
        e = jnp.exp(s - m)
        e_new = jnp.exp(s_new - m)
        l = jnp.sum(e, axis=1, keepdims=True) + e_new
        acc = _dot_nt(_bf(e), _bf(buf[slot, g, pl.ds(128, 128), :])) + _bf(e_new).astype(F32) * v_new
        out = out + jnp.where(head // NSA_HPG == g, (gate[:, 1:2] / l) * acc, 0.0)
    o8_ref[0] = out


def _s_select(page_table, idx2, cache, q3, g3, new4, o8, ta, tb, bn):
    db = page_table.shape[0]
    n_sel = idx2.shape[0] // NSA_GROUPS
    one = lambda *tail: pl.BlockSpec((1,) + tail, lambda b, pt, ix: (b,) + (0,) * len(tail))
    fullp = lambda shape: pl.BlockSpec(shape, lambda b, pt, ix: (0,) * len(shape))
    return pl.pallas_call(
        functools.partial(_s_select_body, n_sel),
        grid_spec=pltpu.PrefetchScalarGridSpec(
            num_scalar_prefetch=2,
            grid=(db,),
            in_specs=[pl.BlockSpec(memory_space=pl.ANY), one(NSA_HEADS, NSA_DK), one(NSA_HEADS, 3),
                      one(1, 512), one(NSA_HEADS, 128), fullp(ta.shape), fullp(tb.shape), fullp(bn.shape)],
            out_specs=one(NSA_HEADS, 128),
            scratch_shapes=[pltpu.VMEM((2, NSA_GROUPS, 256, n_sel * PAGE), F32),
                            pltpu.SemaphoreType.DMA((2,))],
        ),
        out_shape=jax.ShapeDtypeStruct((db, NSA_HEADS, 128), F32),
        compiler_params=_cparams(1),
        name="sample_select",
    )(page_table, idx2, cache, q3, g3, new4, o8, ta, tb, bn)


MLA_S_PAGES = 32


def _s_mla_body(pt_ref, cache_ref, q_ref, knew_ref, o_ref, buf, sem, m_ref, l_ref, acc_ref):
    b, c = pl.program_id(0), pl.program_id(1)
    ncb = pl.num_programs(1)
    n_pages = buf.shape[2] // PAGE
    n = b * ncb + c
    slot = n % 2

    def copy(step, sl, p):
        return pltpu.make_async_copy(cache_ref.at[pt_ref[step // ncb, (step % ncb) * n_pages + p]],
                                     buf.at[sl, :, pl.ds(p * PAGE, PAGE)], sem.at[sl])

    def start(step, sl):
        for p in range(n_pages):
            copy(step, sl, p).start()

    @pl.when(n == 0)
    def _():
        start(0, 0)

    @pl.when(n + 1 < pl.num_programs(0) * ncb)
    def _():
        start(n + 1, 1 - slot)

    @pl.when(c == 0)
    def _():
        m_ref[...] = jnp.full(m_ref.shape, M_INIT, F32)
        l_ref[...] = jnp.zeros(l_ref.shape, F32)
        acc_ref[...] = jnp.zeros(acc_ref.shape, F32)

    for p in range(n_pages):
        copy(n, slot, p).wait()
    q = q_ref[0][:, :MLA_ROW]
    kt = _bf(buf[slot])
    s = _dot(q, kt) * MLA_SCALE
    m_old = m_ref[...]
    m_new = jnp.maximum(m_old, jnp.max(s, axis=1, keepdims=True))
    alpha = jnp.exp(m_old - m_new)
    e = jnp.exp(s - m_new)
    l_ref[...] = alpha * l_ref[...] + jnp.sum(e, axis=1, keepdims=True)
    acc_ref[...] = alpha * acc_ref[...] + _dot_nt(_bf(e), kt[:KV_LORA])
    m_ref[...] = m_new

    @pl.when(c == ncb - 1)
    def _():
        k_new = knew_ref[0]
        s_new = jnp.sum(q_ref[0].astype(F32) * k_new.astype(F32), axis=1, keepdims=True) * MLA_SCALE
        m_old = m_ref[...]
        m_new = jnp.maximum(m_old, s_new)
        alpha = jnp.exp(m_old - m_new)
        e_new = jnp.exp(s_new - m_new)
        l = alpha * l_ref[...] + e_new
        acc = alpha * acc_ref[...] + _bf(e_new).astype(F32) * k_new[:, :KV_LORA].astype(F32)
        o_ref[0] = _bf(acc / l)


def _s_mla(page_table, cache, q3, knew):
    db, n_pages = page_table.shape
    step_pages = min(MLA_S_PAGES, n_pages)
    ncb = n_pages // step_pages
    one = lambda *tail: pl.BlockSpec((1,) + tail, lambda b, c, pt: (b,) + (0,) * len(tail))
    return pl.pallas_call(
        _s_mla_body,
        grid_spec=pltpu.PrefetchScalarGridSpec(
            num_scalar_prefetch=1,
            grid=(db, ncb),
            in_specs=[pl.BlockSpec(memory_space=pl.ANY), one(MLA_HEADS, 256), one(1, 256)],
            out_specs=one(MLA_HEADS, KV_LORA),
            scratch_shapes=[pltpu.VMEM((2, MLA_ROW, step_pages * PAGE), F32),
                            pltpu.SemaphoreType.DMA((2,)), pltpu.VMEM((MLA_HEADS, 1), F32),
                            pltpu.VMEM((MLA_HEADS, 1), F32), pltpu.VMEM((MLA_HEADS, KV_LORA), F32)],
        ),
        out_shape=jax.ShapeDtypeStruct((db, MLA_HEADS, KV_LORA), BF16),
        compiler_params=_cparams(2),
        name="sample_mla",
    )(page_table, cache, q3, knew)


def _s_mem_body(q_ref, kv_ref, o_ref):
    q = q_ref[0]
    head = lax.broadcasted_iota(jnp.int32, (MEM_HEADS, 1), 0)
    out = jnp.zeros((MEM_HEADS, MEM_DH), F32)
    for h in range(MEM_HEADS):
        k = _bf(kv_ref[0, :, 0, h, :])
        v = _bf(kv_ref[0, :, 1, h, :])
        s = _dot_nt(q, k) * (MEM_DH ** -0.5)
        e = jnp.exp(s - jnp.max(s, axis=1, keepdims=True))
        p = e / jnp.sum(e, axis=1, keepdims=True)
        out = jnp.where(head == h, _dot(_bf(p), v), out)
    o_ref[0] = _bf(out)


def _s_mem(mq3, cache_mem):
    db = mq3.shape[0]
    one = lambda *tail: pl.BlockSpec((1,) + tail, lambda b: (b,) + (0,) * len(tail))
    return pl.pallas_call(
        _s_mem_body,
        grid=(db,),
        in_specs=[one(MEM_HEADS, MEM_DH), one(MEM_TOKENS, 2, MEM_HEADS, MEM_DH)],
        out_specs=one(MEM_HEADS, MEM_DH),
        out_shape=jax.ShapeDtypeStruct((db, MEM_HEADS, MEM_DH), BF16),
        compiler_params=_cparams(1),
        name="sample_mem",
    )(mq3, cache_mem)


def _sample_path(x_sample, cache_nsa_kv, cache_mla, state_nsa_win, cache_mem_kv, page_table, rel_bias,
                 ln_g, ln_b, ffn_w, pw, scw, mw):
    db = x_sample.shape[0]
    n_pages = page_table.shape[1]
    past = n_pages * PAGE
    wb = state_nsa_win.shape[2]
    n_blk = (past + 1 + SEL_BLOCK - 1) // SEL_BLOCK
    ln = lambda k: (ln_g[0, k][None], ln_b[0, k][None])
    x1 = _ffn(x_sample[:, 0], *ffn_w[0], *ln(0), db)
    qn, nkv4, win, gate, qmla, row, kmla, mq = _proj(
        x1, pw, _rope_tables(jnp.full((db,), past, jnp.int32)), db, False)
    q3 = qn.reshape(db, NSA_HEADS, NSA_DK)
    g3 = gate[:, :24].reshape(db, 3, NSA_HEADS).transpose(0, 2, 1)
    cache_t = jnp.transpose(cache_nsa_kv[0], (0, 2, 3, 4, 1)).reshape(-1, 512, PAGE)
    mla_t = jnp.transpose(cache_mla[0], (0, 2, 1))
    state_t = jnp.transpose(state_nsa_win[0], (0, 2, 3, 4, 1)).reshape(db, 256, wb)
    nc = past // CMP_BLOCK
    nbk = PAGE // CMP_BLOCK
    bc = _rel_bias(rel_bias, past - CMP_BLOCK * jnp.arange(nc) - (CMP_BLOCK - 1))
    bc = bc.reshape(NSA_HEADS, n_pages, nbk).transpose(0, 2, 1).reshape(NSA_HEADS, nc)
    bw = _rel_bias(rel_bias, wb - jnp.arange(wb))
    bn = _rel_bias(rel_bias, jnp.zeros((1,), jnp.int32))
    near = _rel_bias(rel_bias, PAGE - jnp.arange(PAGE))
    upper = jnp.arange(PAGE) >= SEL_BLOCK
    ta = jnp.where(upper, near, 0.0)
    tb = jnp.where(upper, 0.0, near)
    cmp = _s_compress(page_table, cache_t, scw)
    o8, imp = _s_cmp_win(cmp, q3, g3, state_t, win.reshape(db, 1, 256), bc, bw, bn)
    imp = imp.reshape(db, NSA_GROUPS, nbk, n_pages).transpose(0, 1, 3, 2).reshape(db, NSA_GROUPS, nc)
    idx = _s_topk(imp, n_blk)
    o8 = _s_select(page_table, idx.reshape(-1, db), cache_t, q3, g3, nkv4.reshape(db, 1, 512), o8, ta, tb, bn)
    o_nsa = o8.reshape(db, NSA_HEADS, NSA_GROUPS, NSA_DK)[:, jnp.arange(NSA_HEADS),
                                                           jnp.arange(NSA_HEADS) // NSA_HPG]
    o_nsa = o_nsa.reshape(db, 512).astype(BF16)
    o_lat = _s_mla(page_table, mla_t, qmla.reshape(db, MLA_HEADS, 256), kmla.reshape(db, 1, 256))
    o_mem = _s_mem(mq.reshape(db, MEM_HEADS, MEM_DH), cache_mem_kv[0])
    x2 = _merge(x1, o_nsa, o_lat.reshape(db, -1), o_mem.reshape(db, -1), mw, *ln(1), db)
    y = _ffn(x2, *ffn_w[1], *ln(2), db)
    new_win_t = jnp.concatenate([state_t[:, :, 1:], win[:, :, None]], axis=2)
    new_win = jnp.transpose(new_win_t.reshape(db, 2, NSA_GROUPS, NSA_DK, wb), (0, 4, 1, 2, 3))
    return (y[:, None],
            nkv4.reshape(1, db, 1, 4, NSA_GROUPS, NSA_DK),
            row.reshape(1, db, 1, MLA_ROW),
            new_win[None])


def kernel(x_prompt, x_sample, mem_prompt, cache_nsa_kv, cache_mla, state_nsa_win, cache_mem_kv, page_table, rel_bias, ln_g, ln_b, ffn_w1, ffn_w3, ffn_w2, w_in, nsa_cmp_pos, nsa_cmp_w1, nsa_cmp_w2, mla_g_q, mla_w_uq, mla_w_qr, mla_g_kv, mla_w_uk, mla_w_uv, mem_w_kv, w_br, w_o):
    assert ffn_w1.shape[0] == 1 and x_prompt.shape[0] == 1 and x_sample.shape[1] == 1
    ffn_w = [tuple(w[0, s].astype(BF16) for w in (ffn_w1, ffn_w3, ffn_w2)) for s in range(2)]
    pw = _proj_weights(w_in[0], mla_g_q[0], mla_w_uq[0], mla_w_qr[0], mla_g_kv[0], mla_w_uk[0])
    cw = _compress_weights(nsa_cmp_pos[0], nsa_cmp_w1[0], nsa_cmp_w2[0])
    mw = _merge_weights(w_in[0], mla_w_uv[0], w_br[0], w_o[0])
    yp, p_nsa, p_mla, p_win, p_mem = _prompt_path(x_prompt, mem_prompt, rel_bias, ln_g, ln_b, ffn_w, pw,
                                                  cw, mw, mem_w_kv)
    scw = _sample_compress_weights(nsa_cmp_pos[0], nsa_cmp_w1[0], nsa_cmp_w2[0])
    ys, s_nsa, s_mla, s_win = _sample_path(x_sample, cache_nsa_kv, cache_mla, state_nsa_win, cache_mem_kv,
                                           page_table, rel_bias, ln_g, ln_b, ffn_w, pw, scw, mw)
    return (yp, ys, p_nsa, p_mla, p_win, p_mem, s_nsa, s_mla, s_win)
```

```python
import functools
import math

import jax
import jax.numpy as jnp
import numpy as np
from jax import lax
from jax.experimental import pallas as pl
from jax.experimental.pallas import tpu as pltpu

F32 = jnp.float32
BF16 = jnp.bfloat16

D_MODEL = 1024
D_FF = 2816
NSA_HEADS = 8
NSA_GROUPS = 2
NSA_HPG = 4
NSA_DK = 64
CMP_BLOCK = 32
CMP_HIDDEN = 128
SEL_BLOCK = 64
N_SEL = 16
WINDOW = 512
MLA_HEADS = 8
Q_LORA = 256
KV_LORA = 128
D_NOPE = 64
D_ROPE = 32
D_V = 64
ROPE_THETA = 10000.0
MLA_ROW = KV_LORA + D_ROPE
MEM_TOKENS = 256
MEM_HEADS = 4
MEM_DH = 128
N_BRANCH = 3
BRANCH_W = 512
N_BUCKETS = 32
MAX_DISTANCE = 128
PAGE = 128
ALPHA = 2.0 ** 0.25
LN_EPS = 1e-5
RMS_EPS = 1e-6
IN_WIDTHS = (512, 768, 24, 256, 128, 32, 512, 3072)

LANES = 128
BF16_ROWS = 16
FF_CHUNK = 256
NEG = -1e30
M_INIT = -1e29
VMEM_LIMIT = 56 * 1024 * 1024


def _cparams(n_axes):
    return pltpu.CompilerParams(dimension_semantics=("arbitrary",) * n_axes,
                                vmem_limit_bytes=VMEM_LIMIT)


def _full(shape):
    n = len(shape)
    return pl.BlockSpec(shape, lambda *_: (0,) * n)


def _dot(a, b):
    return jnp.dot(a, b, preferred_element_type=F32)


def _dot_nt(a, b):
    return lax.dot_general(a, b, (((1,), (1,)), ((), ())), preferred_element_type=F32)


def _layer_norm(y, g, b):
    mu = jnp.mean(y, axis=-1, keepdims=True)
    yc = y - mu
    var = jnp.mean(yc * yc, axis=-1, keepdims=True)
    return yc * lax.rsqrt(var + LN_EPS) * g + b


def _ffn_body(x_ref, w1_ref, w3_ref, w2_ref, g_ref, b_ref, o_ref):
    x = x_ref[...]
    xb = x.astype(BF16)
    acc = jnp.zeros(x.shape, F32)
    for c in range(D_FF // FF_CHUNK):
        sl = slice(c * FF_CHUNK, (c + 1) * FF_CHUNK)
        a = _dot(xb, w1_ref[:, sl])
        b = _dot(xb, w3_ref[:, sl])
        h = (a * jax.nn.sigmoid(a) * b).astype(BF16)
        acc = acc + _dot(h, w2_ref[sl, :])
    o_ref[...] = _layer_norm(ALPHA * x + 0.5 * acc, g_ref[...], b_ref[...])


def _ffn(x, w1, w3, w2, g, b, tm):
    rows = x.shape[0]
    return pl.pallas_call(
        _ffn_body,
        grid=(rows // tm,),
        in_specs=[pl.BlockSpec((tm, D_MODEL), lambda i: (i, 0)),
                  _full(w1.shape), _full(w3.shape), _full(w2.shape), _full(g.shape), _full(b.shape)],
        out_specs=pl.BlockSpec((tm, D_MODEL), lambda i: (i, 0)),
        out_shape=jax.ShapeDtypeStruct((rows, D_MODEL), F32),
        compiler_params=_cparams(1),
        name="ffn",
    )(x, w1, w3, w2, g, b)


P_NQ, P_NKV, P_NG, P_QD, P_KVD, P_KR, P_MQ = 0, 512, 1280, 1408, 1664, 1792, 1920
P_WIDTH = 2432


def _rope(x, cos, s_lo, s_hi):
    return (x * cos + pltpu.roll(x, LANES - D_ROPE // 2, 1) * s_lo
            + pltpu.roll(x, D_ROPE // 2, 1) * s_hi)


def _rms(x, g):
    return x * lax.rsqrt(jnp.mean(x * x, axis=-1, keepdims=True) + RMS_EPS) * g


def _proj_body(prompt, x_ref, wp_ref, wuq_ref, wuk_ref, wqr_ref, gq_ref, gkv_ref,
               cos_ref, slo_ref, shi_ref, *outs):
    if prompt:
        (qn_ref, nkv4_ref, win_ref, gate_ref, qmla_ref, row_ref, kmla_ref, mq_ref,
         kslc_ref, vslct_ref, kwin_ref, vwint_ref, ckvt_ref) = outs
    else:
        qn_ref, nkv4_ref, win_ref, gate_ref, qmla_ref, row_ref, kmla_ref, mq_ref = outs
    xb = x_ref[...].astype(BF16)
    cos, s_lo, s_hi = cos_ref[...], slo_ref[...], shi_ref[...]

    def seg(start, width):
        return _dot(xb, wp_ref[:, start:start + width])

    hq = (seg(P_NQ, 512) * (NSA_DK ** -0.5)).astype(BF16)
    if prompt:
        for h in range(NSA_HEADS):
            qn_ref[h] = hq[:, h * NSA_DK:(h + 1) * NSA_DK]
    else:
        qn_ref[...] = hq
    nkv = seg(P_NKV, 768)
    nkv4_ref[...] = nkv[:, :512]
    win_ref[...] = nkv[:, 512:768]
    if prompt:
        tm = nkv.shape[0]
        nkvb = nkv.astype(BF16)
        for g in range(NSA_GROUPS):
            kslc_ref[g] = nkvb[:, 256 + g * NSA_DK:256 + (g + 1) * NSA_DK]
            kwin_ref[g] = nkvb[:, 512 + g * NSA_DK:512 + (g + 1) * NSA_DK]
        vt = nkv[:, 384:512].T.astype(BF16)
        for c in range(tm // 256):
            vslct_ref[c] = vt[:, c * 256:(c + 1) * 256]
        wt = nkv[:, 640:768].T.astype(BF16)
        for c in range(tm // 128):
            vwint_ref[c] = wt[:, c * 128:(c + 1) * 128]
    gate_ref[...] = jax.nn.sigmoid(seg(P_NG, 128))
    cq = _rms(seg(P_QD, 256), gq_ref[...]).astype(BF16)
    q_nope = _dot(cq, wuq_ref[...]).astype(BF16)
    q_lat = _dot(q_nope, wuk_ref[...])
    q_rope = _dot(cq, wqr_ref[...])
    for h in range(MLA_HEADS):
        sl = slice(h * LANES, (h + 1) * LANES)
        qmla_ref[:, 2 * h * LANES:(2 * h + 1) * LANES] = q_lat[:, sl].astype(BF16)
        qmla_ref[:, (2 * h + 1) * LANES:(2 * h + 2) * LANES] = _rope(q_rope[:, sl], cos, s_lo, s_hi).astype(BF16)
    ckv = _rms(seg(P_KVD, 128), gkv_ref[...])
    kr = _rope(seg(P_KR, 128), cos, s_lo, s_hi)
    row_ref[:, :KV_LORA] = ckv
    row_ref[:, KV_LORA:] = kr[:, :D_ROPE]
    kmla_ref[:, :KV_LORA] = ckv.astype(BF16)
    kmla_ref[:, KV_LORA:] = kr.astype(BF16)
    if prompt:
        ckvt = ckv.T.astype(BF16)
        for c in range(ckvt.shape[1] // MLA_KC):
            ckvt_ref[c] = ckvt[:, c * MLA_KC:(c + 1) * MLA_KC]
    mq_ref[...] = seg(P_MQ, 512).astype(BF16)


def _proj(x, wts, tables, tm, prompt):
    rows = x.shape[0]
    wp, wuq, wuk, wqr, gq, gkv = wts
    cos, s_lo, s_hi = tables
    row_spec = lambda w: pl.BlockSpec((tm, w), lambda i: (i, 0))
    out_shapes = [
        jax.ShapeDtypeStruct((NSA_HEADS, rows, NSA_DK) if prompt else (rows, 512), BF16),
        jax.ShapeDtypeStruct((rows, 512), F32),
        jax.ShapeDtypeStruct((rows, 256), F32),
        jax.ShapeDtypeStruct((rows, LANES), F32),
        jax.ShapeDtypeStruct((rows, 2048), BF16),
        jax.ShapeDtypeStruct((rows, MLA_ROW), F32),
        jax.ShapeDtypeStruct((rows, 256), BF16),
        jax.ShapeDtypeStruct((rows, 512), BF16),
    ]
    out_specs = [
        pl.BlockSpec((NSA_HEADS, tm, NSA_DK), lambda i: (0, i, 0)) if prompt else row_spec(512),
        row_spec(512), row_spec(256), row_spec(LANES), row_spec(2048), row_spec(MLA_ROW),
        row_spec(256), row_spec(512),
    ]
    if prompt:
        out_shapes += [
            jax.ShapeDtypeStruct((NSA_GROUPS, rows, NSA_DK), BF16),
            jax.ShapeDtypeStruct((rows // 256, 128, 256), BF16),
            jax.ShapeDtypeStruct((NSA_GROUPS, rows, NSA_DK), BF16),
            jax.ShapeDtypeStruct((rows // 128, 128, 128), BF16),
            jax.ShapeDtypeStruct((rows // MLA_KC, KV_LORA, MLA_KC), BF16),
        ]
        out_specs += [
            pl.BlockSpec((NSA_GROUPS, tm, NSA_DK), lambda i: (0, i, 0)),
            pl.BlockSpec((tm // 256, 128, 256), lambda i: (i, 0, 0)),
            pl.BlockSpec((NSA_GROUPS, tm, NSA_DK), lambda i: (0, i, 0)),
            pl.BlockSpec((tm // 128, 128, 128), lambda i: (i, 0, 0)),
            pl.BlockSpec((tm // MLA_KC, KV_LORA, MLA_KC), lambda i: (i, 0, 0)),
        ]
    return pl.pallas_call(
        functools.partial(_proj_body, prompt),
        grid=(rows // tm,),
        in_specs=[row_spec(D_MODEL), _full(wp.shape), _full(wuq.shape), _full(wuk.shape),
                  _full(wqr.shape), _full(gq.shape), _full(gkv.shape),
                  row_spec(LANES), row_spec(LANES), row_spec(LANES)],
        out_specs=out_specs,
        out_shape=out_shapes,
        compiler_params=_cparams(1),
        name="proj",
    )(x, wp, wuq, wuk, wqr, gq, gkv, cos, s_lo, s_hi)


def _rope_tables(pos):
    half = D_ROPE // 2
    freq = ROPE_THETA ** (-jnp.arange(half, dtype=F32) / half)
    ang = pos.astype(F32)[:, None] * freq
    cos, sin = jnp.cos(ang), jnp.sin(ang)
    z = jnp.zeros((pos.shape[0], LANES - D_ROPE), F32)
    zh = jnp.zeros_like(sin)
    return (jnp.concatenate([cos, cos, z], 1), jnp.concatenate([-sin, zh, z], 1),
            jnp.concatenate([zh, sin, z], 1))


def _proj_weights(w_in, g_q, w_uq, w_qr, g_kv, w_uk):
    offs = np.cumsum((0,) + IN_WIDTHS)
    col = lambda i: w_in[:, offs[i]:offs[i + 1]]
    pad = lambda a, w: jnp.pad(a, ((0, 0), (0, w - a.shape[1])))
    ng = col(2).reshape(D_MODEL, NSA_HEADS, 3).transpose(0, 2, 1).reshape(D_MODEL, 24)
    wp = jnp.concatenate([col(0), col(1), pad(ng, 128), col(3), col(4), pad(col(5), 128), col(6)], 1)
    wuq = w_uq.reshape(Q_LORA, MLA_HEADS * D_NOPE)
    eye = jnp.eye(MLA_HEADS, dtype=F32)
    wuk = jnp.einsum('rhd,hg->hdgr', w_uk, eye).reshape(MLA_HEADS * D_NOPE, MLA_HEADS * KV_LORA)
    wqr = jnp.pad(w_qr, ((0, 0), (0, 0), (0, LANES - D_ROPE))).reshape(Q_LORA, MLA_HEADS * LANES)
    return (wp.astype(BF16), wuq.astype(BF16), wuk.astype(BF16), wqr.astype(BF16),
            g_q.reshape(1, Q_LORA), g_kv.reshape(1, KV_LORA))


def _compress_weights(cmp_pos, cmp_w1, cmp_w2):
    eye = jnp.eye(NSA_GROUPS, dtype=F32)
    w1r = cmp_w1.reshape(2, CMP_BLOCK, NSA_DK, CMP_HIDDEN)
    w1 = jnp.einsum('ktdc,gG->ktgdGc', w1r, eye).reshape(2, CMP_BLOCK, 128, 256)
    w2 = jnp.einsum('kcd,gG->kgcGd', cmp_w2, eye).reshape(2, 256, 128)
    pos = jnp.concatenate([cmp_pos, cmp_pos], axis=-1)
    return pos, w1.astype(BF16), w2.astype(BF16)


def _compress_rows(x_ref, kv, nblk, pos_ref, w1_ref, w2_ref, transposed=False):
    acc = jnp.zeros((nblk, 256), F32)
    for t in range(CMP_BLOCK):
        xt = x_ref[pl.ds(t, nblk, stride=CMP_BLOCK), :]
        acc = acc + _dot((xt + pos_ref[kv, t:t + 1, :]).astype(BF16), w1_ref[kv, t])
    h = (acc * jax.nn.sigmoid(acc)).astype(BF16)
    if transposed:
        return _dot_nt(w2_ref[kv], h)
    return _dot(h, w2_ref[kv])


def _compress_prompt_body(xk_ref, xv_ref, pos_ref, w1_ref, w2_ref, w2t_ref, kc_ref, vct_ref):
    nblk = xk_ref.shape[0] // CMP_BLOCK
    kcb = _compress_rows(xk_ref, 0, nblk, pos_ref, w1_ref, w2_ref).astype(BF16)
    vt = _compress_rows(xv_ref, 1, nblk, pos_ref, w1_ref, w2t_ref, transposed=True).astype(BF16)
    for g in range(NSA_GROUPS):
        kc_ref[g] = kcb[:, g * NSA_DK:(g + 1) * NSA_DK]
        vct_ref[g] = vt[g * NSA_DK:(g + 1) * NSA_DK, :]


def _compress_prompt(nkv4, cw):
    t = nkv4.shape[0]
    rows = min(t, 4096)
    nblk, nc = rows // CMP_BLOCK, t // CMP_BLOCK
    pos, w1, w2 = cw
    w2t = w2.transpose(0, 2, 1)
    return pl.pallas_call(
        _compress_prompt_body,
        grid=(t // rows,),
        in_specs=[pl.BlockSpec((rows, 128), lambda i: (i, 0)), pl.BlockSpec((rows, 128), lambda i: (i, 1)),
                  _full(pos.shape), _full(w1.shape), _full(w2.shape), _full(w2t.shape)],
        out_specs=[pl.BlockSpec((NSA_GROUPS, nblk, NSA_DK), lambda i: (0, i, 0)),
                   pl.BlockSpec((NSA_GROUPS, NSA_DK, nblk), lambda i: (0, 0, i))],
        out_shape=[jax.ShapeDtypeStruct((NSA_GROUPS, nc, NSA_DK), BF16),
                   jax.ShapeDtypeStruct((NSA_GROUPS, NSA_DK, nc), BF16)],
        compiler_params=_cparams(1),
        name="compress_prompt",
    )(nkv4, nkv4, pos, w1, w2, w2t)


def _t5_bucket(dist):
    n = jnp.maximum(dist, 0)
    max_exact = N_BUCKETS // 2
    nf = jnp.maximum(n, 1).astype(F32)
    large = max_exact + (jnp.log(nf / max_exact) / math.log(MAX_DISTANCE / max_exact)
                         * (N_BUCKETS - max_exact)).astype(jnp.int32)
    large = jnp.minimum(large, N_BUCKETS - 1)
    return jnp.where(n < max_exact, n, large)


FAR_DIST = 129


def _rel_bias(rel_bias, dist):
    onehot = jax.nn.one_hot(_t5_bucket(dist), N_BUCKETS, dtype=F32)
    b = jnp.einsum('...b,bh->...h', onehot, rel_bias - rel_bias[N_BUCKETS - 1],
                   precision=lax.Precision.HIGHEST)
    b = jnp.where((dist >= 0)[..., None], b, NEG)
    return jnp.moveaxis(b, -1, 0)


def _lanes_hq(b):
    k = b.shape[1]
    return b.reshape(NSA_GROUPS, NSA_HPG, k, 128).transpose(0, 2, 1, 3).reshape(NSA_GROUPS, k, 512)


def _prompt_bias_tables(rel_bias):
    q = jnp.arange(128)[None, :]
    k = jnp.arange(128)[:, None]
    zero = jnp.zeros((NSA_GROUPS, 128, 512), F32)
    sub = _lanes_hq(_rel_bias(rel_bias, 128 + q - k))
    diag = _lanes_hq(_rel_bias(rel_bias, q - k))
    neg = jnp.full((NSA_GROUPS, 128, 512), NEG, F32)
    anti = _lanes_hq(jnp.broadcast_to(jnp.where(k >= q, 0.0, NEG)[None], (8, 128, 128)))
    near = jnp.stack([zero, sub, diag, neg, anti], axis=1)
    r = jnp.arange(16)[:, None]
    cmpw = jnp.stack([_lanes_hq(_rel_bias(rel_bias, q - CMP_BLOCK * (r - off) - (CMP_BLOCK - 1)))
                      for off in (4, 8, 0)], axis=1)
    return near, cmpw


T_ZERO, T_SUB, T_DIAG, T_NEG, T_ANTI = range(5)


def _softmax_update(s, vt, m_old, l_old, acc_old, scale=None):
    m_new = jnp.maximum(m_old, jnp.max(s, axis=0, keepdims=True))
    if scale is None:
        alpha = jnp.exp(m_old - m_new)
        e = jnp.exp(s - m_new)
    else:
        c = scale * math.log2(math.e)
        alpha = jnp.exp2((m_old - m_new) * c)
        e = jnp.exp2((s - m_new) * c)
    d = vt.shape[0]
    pv = _dot(jnp.concatenate([vt, jnp.ones((BF16_ROWS, vt.shape[1]), BF16)], axis=0), e.astype(BF16))
    return m_new, alpha * l_old + pv[d:d + 1], alpha * acc_old + pv[:d]


def _pipelined(n_groups, chunks, qk, finish, carry, rows, scale=None):
    carry = list(carry)
    raws = [qk(g, chunks[0][0], len(chunks)) for g in range(n_groups)]
    for u, chunk in enumerate(chunks):
        for g in range(n_groups):
            raw = raws[g][u * rows:(u + 1) * rows]
            carry[g] = _softmax_update(*finish(g, chunk, raw), *carry[g], scale=scale)
    return tuple(carry)


FAR_UNROLL = 4


def _far_loop(n_groups, n_chunks, rows, qk, finish, init, raw_a, raw_b, scale=None):
    u_n = FAR_UNROLL
    n_pairs = n_chunks // (2 * u_n)

    def consume(raw_ref, j0, carry):
        carry = list(carry)
        for u in range(u_n):
            for g in range(n_groups):
                raw = raw_ref[g, pl.ds(u * rows, rows), :]
                carry[g] = _softmax_update(*finish(g, (j0 + u, None, None), raw), *carry[g], scale=scale)
        return tuple(carry)

    @pl.when(n_pairs > 0)
    def _():
        for g in range(n_groups):
            raw_a[g] = qk(g, 0, u_n)

    def pair(pp, carry):
        base = 2 * u_n * pp
        for g in range(n_groups):
            raw_b[g] = qk(g, base + u_n, u_n)
        carry = consume(raw_a, base, carry)
        nxt = jnp.minimum(pp + 1, n_pairs - 1)
        for g in range(n_groups):
            raw_a[g] = qk(g, 2 * u_n * nxt, u_n)
        return consume(raw_b, base + u_n, carry)

    st = lax.fori_loop(0, n_pairs, pair, init)
    run = functools.partial(_pipelined, n_groups, qk=qk, finish=finish, rows=rows, scale=scale)
    done = 2 * u_n * n_pairs
    quads = (n_chunks - done) // u_n
    st = lax.fori_loop(0, quads,
                       lambda q, c: run([(done + u_n * q + u, None, None) for u in range(u_n)], carry=c), st)
    return lax.fori_loop(done + u_n * quads, n_chunks, lambda j, c: run([(j, None, None)], carry=c), st)


def _topk_rows(score, n_sel):
    nb = score.shape[0]
    blk = lax.broadcasted_iota(jnp.int32, score.shape, 0)
    sel = jnp.zeros(score.shape, F32)
    picks = []
    for _ in range(n_sel):
        m = jnp.max(score, axis=0, keepdims=True)
        j = jnp.min(jnp.where(score == m, blk, nb), axis=0, keepdims=True)
        hit = blk == j
        sel = jnp.where(hit, 1.0, sel)
        score = jnp.where(hit, -2.0, score)
        picks.append(j)
    return sel, picks


def _nsa_prompt_body(q_ref, gate_ref, kc_ref, vct_ref, kslc_ref, vslct_ref, kwin_ref, vwint_ref,
                     near_ref, cmpw_ref, o_ref, s_ref, imp_ref, sel_ref, m_ref, l_ref, acc_ref, out_ref,
                     rawa_ref, rawb_ref):
    i = pl.program_id(0)
    nc = kc_ref.shape[1]
    nb = nc // 2
    gate_t = gate_ref[...].T
    lane_q = lax.broadcasted_iota(jnp.int32, (1, 128), 1)
    cur = 2 * i + (lane_q >= SEL_BLOCK).astype(jnp.int32)
    odd = i % 2
    w0 = pl.multiple_of(jnp.where(i == 0, 0, jnp.where(odd == 1, 4 * i - 4, 4 * i - 8)), 8)
    var = jnp.where(i == 0, 2, jnp.where(odd == 1, 0, 1))
    jl = i // 2
    even = 1 - odd

    def reset():
        m_ref[...] = jnp.full(m_ref.shape, M_INIT, F32)
        l_ref[...] = jnp.zeros(l_ref.shape, F32)
        acc_ref[...] = jnp.zeros(acc_ref.shape, F32)

    def gate_row(g, b):
        return jnp.concatenate([gate_t[b * 8 + 4 * g + h:b * 8 + 4 * g + h + 1, :]
                                for h in range(NSA_HPG)], axis=1)

    groups = range(NSA_GROUPS)

    def q_group(g):
        return q_ref[4 * g:4 * g + 4].reshape(4 * 128, NSA_DK)

    for g in groups:
        s_ref[g, pl.ds(nc, 16), :] = jnp.zeros((16, 512), F32)
        s_ref[g, pl.ds(0, nc), :] = _dot_nt(kc_ref[g], q_group(g))
        s_ref[g, pl.ds(w0, 16), :] = s_ref[g, pl.ds(w0, 16), :] + cmpw_ref[g, var]
        row_n = lax.broadcasted_iota(jnp.int32, (nc, 1), 0)
        s = jnp.where(row_n <= 4 * i + 3, s_ref[g, pl.ds(0, nc), :], NEG)
        m = jnp.maximum(jnp.max(s, axis=0, keepdims=True), M_INIT)
        e = jnp.exp(s - m)
        d = jnp.sum(e, axis=0, keepdims=True)
        p = e / jnp.where(d > 0.0, d, 1.0)
        out_ref[g] = gate_row(g, 0) * _dot(vct_ref[g], p.astype(BF16))
        imp_ref[g] = p[:, 0:128] + p[:, 128:256] + p[:, 256:384] + p[:, 384:512]
        imp = imp_ref[g, pl.ds(0, nb, stride=2), :] + imp_ref[g, pl.ds(1, nb, stride=2), :]
        blk = lax.broadcasted_iota(jnp.int32, (nb, 128), 0)
        valid = blk <= cur
        forced = valid & ((blk == 0) | (blk == cur) | (blk == cur - 1))
        score = jnp.where(forced, NSA_HPG + 1.0, jnp.where(valid, imp, -1.0))
        sel_ref[g], _ = _topk_rows(score, min(N_SEL, nb))

    def sel_qk(g, j0, n):
        rows = kslc_ref[g, pl.ds(pl.multiple_of(j0 * 256, 256), n * 256), :]
        return _dot_nt(rows, q_group(g))

    def sel_scores(g, chunk, s):
        j, top, bot = chunk
        parts = []
        for b in range(4):
            row = sel_ref[g, pl.ds(4 * j + b, 1), :]
            mask = jnp.concatenate([row] * NSA_HPG, axis=1) > 0.5
            sb = s[b * SEL_BLOCK:(b + 1) * SEL_BLOCK]
            if top is not None:
                tab = near_ref[g, top if b < 2 else bot]
                sb = sb + tab[(b % 2) * SEL_BLOCK:(b % 2 + 1) * SEL_BLOCK]
            parts.append(jnp.where(mask, sb, NEG))
        return jnp.concatenate(parts, axis=0), vslct_ref[j, g * NSA_DK:(g + 1) * NSA_DK, :]

    def load_state():
        return tuple((m_ref[g], l_ref[g], acc_ref[g]) for g in groups)

    def store_state(st):
        for g in groups:
            m_ref[g], l_ref[g], acc_ref[g] = st[g]

    sel_run = functools.partial(_pipelined, NSA_GROUPS, qk=sel_qk, finish=sel_scores, rows=256)
    n_far = jnp.maximum(jl - 1, 0)
    init = tuple((jnp.full((1, 512), M_INIT, F32), jnp.zeros((1, 512), F32),
                  jnp.zeros((NSA_DK, 512), F32)) for _ in groups)
    store_state(_far_loop(NSA_GROUPS, n_far, 256, sel_qk, sel_scores, init, rawa_ref, rawb_ref))
    last_tabs = (jl, jnp.where(even == 1, T_DIAG, T_SUB), jnp.where(even == 1, T_NEG, T_DIAG))

    @pl.when(jl >= 1)
    def _():
        prev_tabs = (jl - 1, T_ZERO, jnp.where(even == 1, T_SUB, T_ZERO))
        store_state(sel_run([prev_tabs, last_tabs], carry=load_state()))

    @pl.when(jl < 1)
    def _():
        store_state(sel_run([last_tabs], carry=load_state()))

    for g in groups:
        out_ref[g] = out_ref[g] + (gate_row(g, 1) / l_ref[g]) * acc_ref[g]

    def win_qk(g, j0, n):
        return _dot_nt(kwin_ref[g, pl.ds(pl.multiple_of(j0 * 128, 128), n * 128), :], q_group(g))

    def win_finish(g, chunk, s):
        j, tab = chunk
        if tab is not None:
            s = s + near_ref[g, tab]
        return s, vwint_ref[j, g * NSA_DK:(g + 1) * NSA_DK, :]

    win_run = functools.partial(_pipelined, NSA_GROUPS, qk=win_qk, finish=win_finish, rows=128)
    win_chunks = ((4, T_ANTI), (3, None), (2, None), (1, T_SUB), (0, T_DIAG))
    reset()

    @pl.when(i >= 4)
    def _():
        store_state(win_run([(i - back, tab) for back, tab in win_chunks], carry=load_state()))

    @pl.when(i < 4)
    def _():
        for back, tab in win_chunks:
            @pl.when(i >= back)
            def _(back=back, tab=tab):
                store_state(win_run([(i - back, tab)], carry=load_state()))

    for g in groups:
        out = out_ref[g] + (gate_row(g, 2) / l_ref[g]) * acc_ref[g]
        for h in range(NSA_HPG):
            col = (4 * g + h) * NSA_DK
            o_ref[:, col:col + NSA_DK] = out[:, h * 128:(h + 1) * 128].T.astype(BF16)


def _nsa_prompt(qn, gate, kc, vct, kslc, vslct, kwin, vwint, near, cmpw):
    t = qn.shape[1]
    nc = kc.shape[1]
    return pl.pallas_call(
        _nsa_prompt_body,
        grid=(t // 128,),
        in_specs=[pl.BlockSpec((NSA_HEADS, 128, NSA_DK), lambda i: (0, i, 0)),
                  pl.BlockSpec((128, LANES), lambda i: (i, 0)),
                  _full(kc.shape), _full(vct.shape), _full(kslc.shape), _full(vslct.shape),
                  _full(kwin.shape), _full(vwint.shape), _full(near.shape), _full(cmpw.shape)],
        out_specs=pl.BlockSpec((128, 512), lambda i: (i, 0)),
        out_shape=jax.ShapeDtypeStruct((t, 512), BF16),
        scratch_shapes=[pltpu.VMEM((NSA_GROUPS, nc + 16, 512), F32), pltpu.VMEM((NSA_GROUPS, nc, 128), F32),
                        pltpu.VMEM((NSA_GROUPS, nc // 2, 128), F32), pltpu.VMEM((NSA_GROUPS, 1, 512), F32),
                        pltpu.VMEM((NSA_GROUPS, 1, 512), F32), pltpu.VMEM((NSA_GROUPS, NSA_DK, 512), F32),
                        pltpu.VMEM((NSA_GROUPS, NSA_DK, 512), F32),
                        pltpu.VMEM((NSA_GROUPS, FAR_UNROLL * 256, 512), F32),
                        pltpu.VMEM((NSA_GROUPS, FAR_UNROLL * 256, 512), F32)],
        compiler_params=_cparams(1),
        name="nsa_prompt",
    )(qn, gate, kc, vct, kslc, vslct, kwin, vwint, near, cmpw)


MLA_TQ = 128
MLA_KC = 256
MLA_HALF_HEADS = 4
MLA_SCALE = (D_NOPE + D_ROPE) ** -0.5


def _mla_mask_tables():
    k = np.arange(128)[:, None]
    q = np.arange(128)[None, :]
    diag = np.tile(np.where(k <= q, 0.0, NEG), (1, MLA_HALF_HEADS))
    return jnp.asarray(np.stack([np.zeros_like(diag), diag, np.full_like(diag, NEG)]), F32)


M_ZERO, M_DIAG, M_NEG = range(3)


def _mla_prompt_body(q_ref, k_ref, vt_ref, tab_ref, o_ref, rawa_ref, rawb_ref):
    i = pl.program_id(0)
    halves = range(MLA_HEADS // MLA_HALF_HEADS)
    jl = i // 2
    even = 1 - i % 2

    def q_half(hh):
        return jnp.concatenate([q_ref[:, (hh * MLA_HALF_HEADS + h) * 256:(hh * MLA_HALF_HEADS + h + 1) * 256]
                                for h in range(MLA_HALF_HEADS)], axis=0)

    def qk(hh, j0, n):
        return _dot_nt(k_ref[pl.ds(pl.multiple_of(j0 * MLA_KC, MLA_KC), n * MLA_KC), :], q_half(hh))

    def finish(hh, chunk, s):
        j, top, bot = chunk
        if top is not None:
            s = s + jnp.concatenate([tab_ref[top], tab_ref[bot]], axis=0)
        return s, vt_ref[j]

    run = functools.partial(_pipelined, len(halves), qk=qk, finish=finish, rows=MLA_KC, scale=MLA_SCALE)
    lanes = MLA_HALF_HEADS * MLA_TQ
    init = tuple((jnp.full((1, lanes), M_INIT, F32), jnp.zeros((1, lanes), F32),
                  jnp.zeros((KV_LORA, lanes), F32)) for _ in halves)
    st = _far_loop(len(halves), jl, MLA_KC, qk, finish, init, rawa_ref, rawb_ref, scale=MLA_SCALE)
    st = run([(jl, jnp.where(even == 1, M_DIAG, M_ZERO), jnp.where(even == 1, M_NEG, M_DIAG))], carry=st)
    for hh in halves:
        _, l, acc = st[hh]
        o = acc / l
        for h in range(MLA_HALF_HEADS):
            head = hh * MLA_HALF_HEADS + h
            o_ref[:, head * KV_LORA:(head + 1) * KV_LORA] = o[:, h * MLA_TQ:(h + 1) * MLA_TQ].T.astype(BF16)


def _mla_prompt(qmla, kmla, ckvt):
    t = qmla.shape[0]
    tabs = _mla_mask_tables()
    return pl.pallas_call(
        _mla_prompt_body,
        grid=(t // MLA_TQ,),
        in_specs=[pl.BlockSpec((MLA_TQ, 2048), lambda i: (i, 0)), _full(kmla.shape), _full(ckvt.shape),
                  _full(tabs.shape)],
        out_specs=pl.BlockSpec((MLA_TQ, MLA_HEADS * KV_LORA), lambda i: (i, 0)),
        out_shape=jax.ShapeDtypeStruct((t, MLA_HEADS * KV_LORA), BF16),
        scratch_shapes=[pltpu.VMEM((MLA_HEADS // MLA_HALF_HEADS, FAR_UNROLL * MLA_KC, MLA_HALF_HEADS * MLA_TQ), F32)
                        for _ in range(2)],
        compiler_params=_cparams(1),
        name="mla_prompt",
    )(qmla, kmla, ckvt, tabs)


def _mem_kv_body(x_ref, w_ref, o_ref, ob_ref):
    kv = _dot(x_ref[...].astype(BF16), w_ref[...])
    o_ref[...] = kv
    ob_ref[...] = kv.astype(BF16)


def _mem_kv(mem, w):
    shp = (MEM_TOKENS, 2 * MEM_HEADS * MEM_DH)
    return pl.pallas_call(
        _mem_kv_body,
        in_specs=[_full(mem.shape), _full(w.shape)],
        out_specs=[_full(shp), _full(shp)],
        out_shape=[jax.ShapeDtypeStruct(shp, F32), jax.ShapeDtypeStruct(shp, BF16)],
        grid=(1,),
        compiler_params=_cparams(1),
        name="mem_kv",
    )(mem, w)


def _mem_attn_body(q_ref, kv_ref, o_ref):
    for h in range(MEM_HEADS):
        sl = slice(h * MEM_DH, (h + 1) * MEM_DH)
        k = kv_ref[:, sl]
        v = kv_ref[:, MEM_HEADS * MEM_DH + h * MEM_DH:MEM_HEADS * MEM_DH + (h + 1) * MEM_DH]
        s = _dot_nt(q_ref[:, sl], k) * (MEM_DH ** -0.5)
        e = jnp.exp(s - jnp.max(s, axis=1, keepdims=True))
        p = e / jnp.sum(e, axis=1, keepdims=True)
        o_ref[:, sl] = _dot(p.astype(BF16), v).astype(BF16)


def _mem_attn(mq, kvb, tm):
    t = mq.shape[0]
    return pl.pallas_call(
        _mem_attn_body,
        grid=(t // tm,),
        in_specs=[pl.BlockSpec((tm, 512), lambda i: (i, 0)), _full(kvb.shape)],
        out_specs=pl.BlockSpec((tm, 512), lambda i: (i, 0)),
        out_shape=jax.ShapeDtypeStruct((t, 512), BF16),
        compiler_params=_cparams(1),
        name="mem_attn",
    )(mq, kvb)


def _merge_weights(w_in, w_uv, w_br, w_o):
    wmg = w_in[:, sum(IN_WIDTHS[:-1]):]
    eye = jnp.eye(MLA_HEADS, dtype=F32)
    wuv = jnp.einsum('rhd,hg->hrgd', w_uv, eye).reshape(MLA_HEADS * KV_LORA, MLA_HEADS * D_V)
    return wmg.astype(BF16), wuv.astype(BF16), w_br.astype(BF16), w_o.astype(BF16)


def _merge_body(x_ref, onsa_ref, olat_ref, omem_ref, wmg_ref, wuv_ref, wbr_ref, wo_ref, g_ref, b_ref,
                o_ref):
    x = x_ref[...]
    xb = x.astype(BF16)
    v_mla = _dot(olat_ref[...], wuv_ref[...]).astype(BF16)
    tot = jnp.zeros(x.shape, F32)
    for b, br in enumerate((onsa_ref[...], v_mla, omem_ref[...])):
        gate = jax.nn.sigmoid(_dot(xb, wmg_ref[:, b * D_MODEL:(b + 1) * D_MODEL]))
        tot = tot + gate * _dot(br, wbr_ref[b])
    mix = _dot(tot.astype(BF16), wo_ref[...])
    o_ref[...] = _layer_norm(ALPHA * x + mix, g_ref[...], b_ref[...])


def _merge(x, onsa, olat, omem, mw, g, b, tm):
    rows = x.shape[0]
    wmg, wuv, wbr, wo = mw
    row_spec = lambda w: pl.BlockSpec((tm, w), lambda i: (i, 0))
    return pl.pallas_call(
        _merge_body,
        grid=(rows // tm,),
        in_specs=[row_spec(D_MODEL), row_spec(512), row_spec(1024), row_spec(512),
                  _full(wmg.shape), _full(wuv.shape), _full(wbr.shape), _full(wo.shape),
                  _full(g.shape), _full(b.shape)],
        out_specs=row_spec(D_MODEL),
        out_shape=jax.ShapeDtypeStruct((rows, D_MODEL), F32),
        compiler_params=_cparams(1),
        name="merge",
    )(x, onsa, olat, omem, mw[0], wuv, wbr, wo, g, b)


def _prompt_path(x_prompt, mem_prompt, rel_bias, ln_g, ln_b, ffn_w, pw, cw, mw, mem_w_kv):
    t = x_prompt.shape[1]
    tm = 512
    ln = lambda k: (ln_g[0, k][None], ln_b[0, k][None])
    x1 = _ffn(x_prompt[0], *ffn_w[0], *ln(0), tm)
    (qn, nkv4, win, gate, qmla, row, kmla, mq, kslc, vslct, kwin, vwint, ckvt) = _proj(
        x1, pw, _rope_tables(jnp.arange(t)), tm, True)
    kc, vct = _compress_prompt(nkv4, cw)
    near, cmpw = _prompt_bias_tables(rel_bias)
    o_nsa = _nsa_prompt(qn, gate, kc, vct, kslc, vslct, kwin, vwint, near, cmpw)
    o_lat = _mla_prompt(qmla, kmla, ckvt)
    mem_kv, mem_kvb = _mem_kv(mem_prompt[0], mem_w_kv[0].reshape(D_MODEL, -1).astype(BF16))
    o_mem = _mem_attn(mq, mem_kvb, tm)
    x2 = _merge(x1, o_nsa, o_lat, o_mem, mw, *ln(1), tm)
    y = _ffn(x2, *ffn_w[1], *ln(2), tm)
    wb = min(WINDOW, t)
    return (y[None],
            nkv4.reshape(1, 1, t, 4, NSA_GROUPS, NSA_DK),
            row.reshape(1, 1, t, MLA_ROW),
            win[t - wb:].reshape(1, 1, wb, 2, NSA_GROUPS, NSA_DK),
            mem_kv.reshape(1, 1, MEM_TOKENS, 2, MEM_HEADS, MEM_DH))


def _own_group_lanes(q):
    q2 = jnp.concatenate([q, q], axis=1)
    head = lax.broadcasted_iota(jnp.int32, q2.shape, 0)
    lane = lax.broadcasted_iota(jnp.int32, q2.shape, 1)
    return jnp.where(head // NSA_HPG == lane // NSA_DK, q2, jnp.zeros_like(q2))


def _bf(x):
    return x.astype(BF16)


def _new_key_score(qm, k_new):
    return jnp.sum(qm.astype(F32) * _bf(k_new).astype(F32), axis=1, keepdims=True)


CMP_PITCH = 136


def _sample_compress_weights(cmp_pos, cmp_w1, cmp_w2):
    eye4 = jnp.eye(PAGE // CMP_BLOCK, dtype=F32)
    w1p = cmp_w1.reshape(2, CMP_BLOCK, NSA_DK // 2, 2, CMP_HIDDEN)
    w1 = jnp.einsum('ktpjc,nm->kpjntmc', w1p, eye4).reshape(2, NSA_DK // 2, 256, 512)
    eye2 = jnp.eye(NSA_GROUPS, dtype=F32)
    w2 = jnp.einsum('kcd,gG->kgcGd', cmp_w2, eye2).reshape(2, 256, 128)
    pos = jnp.tile(cmp_pos.transpose(0, 2, 1), (1, 1, PAGE // CMP_BLOCK))
    return pos, w1.astype(BF16), w2.astype(BF16)


def _s_compress_body(pt_ref, cache_ref, pos_ref, w1_ref, w2_ref, out_ref, buf, sem):
    kv, b = pl.program_id(0), pl.program_id(1)
    db = pl.num_programs(1)
    n_pages = pt_ref.shape[1]
    n = kv * db + b
    slot = n % 2

    def copies(step, sl, wait):
        skv, sb = step // db, step % db

        for p in range(n_pages):
            cp = pltpu.make_async_copy(
                cache_ref.at[pt_ref[sb, p], pl.ds(pl.multiple_of(skv * 128, 128), 128), :],
                buf.at[sl, pl.ds(p * CMP_PITCH, 128), :], sem.at[sl])
            cp.wait() if wait else cp.start()

    @pl.when(n == 0)
    def _():
        copies(0, 0, False)

    def compress():
        page_rows = buf.at[slot]
        acc = jnp.zeros((NSA_GROUPS * n_pages, 512), F32)
        for dp in range(NSA_DK // 2):
            rows = []
            for g in range(NSA_GROUPS):
                pair = [page_rows[pl.ds(g * NSA_DK + 2 * dp + j, n_pages, stride=CMP_PITCH), :]
                        + pos_ref[0, 2 * dp + j:2 * dp + j + 1, :] for j in range(2)]
                rows.append(jnp.concatenate(pair, axis=1))
            acc = acc + _dot(_bf(jnp.concatenate(rows, axis=0)), w1_ref[0, dp])
        h = _bf(acc * jax.nn.sigmoid(acc))
        for nb in range(PAGE // CMP_BLOCK):
            hh = jnp.concatenate([h[g * n_pages:(g + 1) * n_pages, nb * 128:(nb + 1) * 128]
                                  for g in range(NSA_GROUPS)], axis=1)
            out_ref[0, 0, nb] = _bf(_dot(hh, w2_ref[0]))

    @pl.when(n + 1 < 2 * db)
    def _():
        copies(n, slot, True)
        copies(n + 1, 1 - slot, False)
        compress()

    @pl.when(n + 1 >= 2 * db)
    def _():
        copies(n, slot, True)
        compress()


def _s_compress(page_table, cache_t, cw):
    db, n_pages = page_table.shape
    pos, w1, w2 = cw
    kvspec = lambda shape: pl.BlockSpec((1,) + shape[1:], lambda kv, b, pt: (kv,) + (0,) * (len(shape) - 1))
    nbk = PAGE // CMP_BLOCK
    return pl.pallas_call(
        _s_compress_body,
        grid_spec=pltpu.PrefetchScalarGridSpec(
            num_scalar_prefetch=1,
            grid=(2, db),
            in_specs=[pl.BlockSpec(memory_space=pl.ANY), kvspec(pos.shape), kvspec(w1.shape), kvspec(w2.shape)],
            out_specs=pl.BlockSpec((1, 1, nbk, n_pages, 128), lambda kv, b, pt: (kv, b, 0, 0, 0)),
            scratch_shapes=[pltpu.VMEM((2, n_pages * CMP_PITCH, 128), F32), pltpu.SemaphoreType.DMA((2,))],
        ),
        out_shape=jax.ShapeDtypeStruct((2, db, nbk, n_pages, 128), BF16),
        compiler_params=_cparams(2),
        name="sample_compress",
    )(page_table, cache_t, pos, w1, w2)


def _s_cmp_win_body(cmp_ref, q_ref, g_ref, st_ref, wnew_ref, bc_ref, bw_ref, bn_ref, o8_ref, imp_ref):
    nc = bc_ref.shape[1]
    kc = cmp_ref[0, 0].reshape(nc, 128)
    vc = cmp_ref[1, 0].reshape(nc, 128)
    qm = _own_group_lanes(q_ref[0])
    gate = g_ref[0]
    s = _dot_nt(qm, kc) + bc_ref[...]
    e = jnp.exp(s - jnp.max(s, axis=1, keepdims=True))
    p = e / jnp.sum(e, axis=1, keepdims=True)
    out = gate[:, 0:1] * _dot(_bf(p), vc)
    for g in range(NSA_GROUPS):
        imp_ref[0, g:g + 1, :] = jnp.sum(p[NSA_HPG * g:NSA_HPG * (g + 1)], axis=0, keepdims=True)
    st = st_ref[0]
    wnew = wnew_ref[0]
    s = _dot(qm, _bf(st[:128])) + bw_ref[...]
    s_new = _new_key_score(qm, wnew[:, :128]) + bn_ref[...]
    m = jnp.maximum(jnp.max(s, axis=1, keepdims=True), s_new)
    e = jnp.exp(s - m)
    e_new = jnp.exp(s_new - m)
    l = jnp.sum(e, axis=1, keepdims=True) + e_new
    acc = _dot_nt(_bf(e), _bf(st[128:])) + _bf(e_new).astype(F32) * _bf(wnew[:, 128:]).astype(F32)
    o8_ref[0] = out + (gate[:, 2:3] / l) * acc


def _s_cmp_win(cmp, q3, g3, state_t, wnew, bc, bw, bn):
    db = q3.shape[0]
    nc = bc.shape[1]
    wb = state_t.shape[2]
    one = lambda *tail: pl.BlockSpec((1,) + tail, lambda b: (b,) + (0,) * len(tail))
    return pl.pallas_call(
        _s_cmp_win_body,
        grid=(db,),
        in_specs=[pl.BlockSpec((2, 1) + cmp.shape[2:], lambda b: (0, b, 0, 0, 0)),
                  one(NSA_HEADS, NSA_DK), one(NSA_HEADS, 3), one(256, wb), one(1, 256),
                  _full(bc.shape), _full(bw.shape), _full(bn.shape)],
        out_specs=[one(NSA_HEADS, 128), one(NSA_GROUPS, nc)],
        out_shape=[jax.ShapeDtypeStruct((db, NSA_HEADS, 128), F32),
                   jax.ShapeDtypeStruct((db, NSA_GROUPS, nc), F32)],
        compiler_params=_cparams(1),
        name="sample_cmp_win",
    )(cmp, q3, g3, state_t, wnew, bc, bw, bn)


def _s_topk_body(n_blk, imp_ref, idx_ref, t_ref):
    db, nc = imp_ref.shape[0], imp_ref.shape[2]
    nbp = nc // 2
    rows = t_ref.shape[0] // 2
    cur = n_blk - 1
    for g in range(NSA_GROUPS):
        t_ref[pl.ds(0, nc), :] = imp_ref[:, g, :].T
        t_ref[pl.ds(nc, t_ref.shape[0] - nc), :] = jnp.zeros((t_ref.shape[0] - nc, db), F32)
        imp = t_ref[pl.ds(0, rows, stride=2), :] + t_ref[pl.ds(1, rows, stride=2), :]
        blk = lax.broadcasted_iota(jnp.int32, (rows, db), 0)
        forced = (blk == 0) | (blk == cur) | (blk == cur - 1)
        score = jnp.where(blk >= n_blk, -2.0, jnp.where(forced, NSA_HPG + 1.0, imp))
        _, picks = _topk_rows(score, min(N_SEL, n_blk))
        idx_ref[g] = jnp.concatenate(picks, axis=0)


def _s_topk(imp, n_blk):
    db, _, nc = imp.shape
    rows = -(-n_blk // 8) * 8
    n_sel = min(N_SEL, n_blk)
    return pl.pallas_call(
        functools.partial(_s_topk_body, n_blk),
        grid=(1,),
        in_specs=[_full(imp.shape)],
        out_specs=_full((NSA_GROUPS, n_sel, db)),
        out_shape=jax.ShapeDtypeStruct((NSA_GROUPS, n_sel, db), jnp.int32),
        scratch_shapes=[pltpu.VMEM((2 * rows, db), F32)],
        compiler_params=_cparams(1),
        name="sample_topk",
    )(imp)


def _s_select_body(n_sel, pt_ref, idx_ref, cache_ref, q_ref, g_ref, new_ref, o8in_ref, ta_ref, tb_ref,
                   bn_ref, o8_ref, buf, sem):
    b = pl.program_id(0)
    nbp = pt_ref.shape[1] * (PAGE // SEL_BLOCK)
    slot = b % 2
    n_dma = NSA_GROUPS * n_sel

    def copy(seq, sl, n):
        blk = jnp.minimum(idx_ref[n, seq], nbp - 1)
        return pltpu.make_async_copy(cache_ref.at[pt_ref[seq, blk // 2], pl.ds(256, 256), :],
                                     buf.at[sl, n // n_sel, :, pl.ds((n % n_sel) * PAGE, PAGE)], sem.at[sl])

    def start(seq, sl):
        for n in range(n_dma):
            copy(seq, sl, n).start()

    @pl.when(b == 0)
    def _():
        start(0, 0)

    @pl.when(b + 1 < pl.num_programs(0))
    def _():
        start(b + 1, 1 - slot)

    for n in range(n_dma):
        copy(b, slot, n).wait()
    qm = _own_group_lanes(q_ref[0])
    gate = g_ref[0]
    new = new_ref[0]
    head = lax.broadcasted_iota(jnp.int32, (NSA_HEADS, 1), 0)
    upper = (lax.broadcasted_iota(jnp.int32, (1, PAGE), 1) >= SEL_BLOCK).astype(F32)
    s_new = _new_key_score(qm, new[:, 256:384]) + bn_ref[...]
    v_new = _bf(new[:, 384:512]).astype(F32)
    out = o8in_ref[0]
    for g in range(NSA_GROUPS):
        pieces = []
        for r in range(n_sel):
            blk = idx_ref[g * n_sel + r, b]
            wa = jnp.where(blk == nbp - 1, 1.0, 0.0)
            wb = jnp.where(blk == nbp - 2, 1.0, 0.0)
            wm = jnp.where(blk >= nbp, NEG, 0.0)
            hb = (jnp.minimum(blk, nbp - 1) % 2).astype(F32)
            other_half = upper + hb - 2.0 * upper * hb
            pieces.append(wa * ta_ref[...] + wb * tb_ref[...] + wm + other_half * NEG)
        s = _dot(qm, _bf(buf[slot, g, pl.ds(0, 128), :])) + jnp.concatenate(pieces, axis=1)
        m = jnp.maximum(jnp.max(s, axis=1, keepdims=True), s_new)
        e = jnp.exp(s - m)
        e_new = jnp.exp(s_new - m)
        l = jnp.sum(e, axis=1, keepdims=True) + e_new
        acc = _dot_nt(_bf(e), _bf(buf[slot, g, pl.ds(128, 128), :])) + _bf(e_new).astype(F32) * v_new
        out = out + jnp.where(head // NSA_HPG == g, (gate[:, 1:2] / l) * acc, 0.0)
    o8_ref[0] = out


def _s_select(page_table, idx2, cache, q3, g3, new4, o8, ta, tb, bn):
    db = page_table.shape[0]
    n_sel = idx2.shape[0] // NSA_GROUPS
    one = lambda *tail: pl.BlockSpec((1,) + tail, lambda b, pt, ix: (b,) + (0,) * len(tail))
    fullp = lambda shape: pl.BlockSpec(shape, lambda b, pt, ix: (0,) * len(shape))
    return pl.pallas_call(
        functools.partial(_s_select_body, n_sel),
        grid_spec=pltpu.PrefetchScalarGridSpec(
            num_scalar_prefetch=2,
            grid=(db,),
            in_specs=[pl.BlockSpec(memory_space=pl.ANY), one(NSA_HEADS, NSA_DK), one(NSA_HEADS, 3),
                      one(1, 512), one(NSA_HEADS, 128), fullp(ta.shape), fullp(tb.shape), fullp(bn.shape)],
            out_specs=one(NSA_HEADS, 128),
            scratch_shapes=[pltpu.VMEM((2, NSA_GROUPS, 256, n_sel * PAGE), F32),
                            pltpu.SemaphoreType.DMA((2,))],
        ),
        out_shape=jax.ShapeDtypeStruct((db, NSA_HEADS, 128), F32),
        compiler_params=_cparams(1),
        name="sample_select",
    )(page_table, idx2, cache, q3, g3, new4, o8, ta, tb, bn)


MLA_S_PAGES = 32


def _s_mla_body(pt_ref, cache_ref, q_ref, knew_ref, o_ref, buf, sem, m_ref, l_ref, acc_ref):
    b, c = pl.program_id(0), pl.program_id(1)
    ncb = pl.num_programs(1)
    n_pages = buf.shape[2] // PAGE
    n = b * ncb + c
    slot = n % 2

    def copy(step, sl, p):
        return pltpu.make_async_copy(cache_ref.at[pt_ref[step // ncb, (step % ncb) * n_pages + p]],
                                     buf.at[sl, :, pl.ds(p * PAGE, PAGE)], sem.at[sl])

    def start(step, sl):
        for p in range(n_pages):
            copy(step, sl, p).start()

    @pl.when(n == 0)
    def _():
        start(0, 0)

    @pl.when(n + 1 < pl.num_programs(0) * ncb)
    def _():
        start(n + 1, 1 - slot)

    @pl.when(c == 0)
    def _():
        m_ref[...] = jnp.full(m_ref.shape, M_INIT, F32)
        l_ref[...] = jnp.zeros(l_ref.shape, F32)
        acc_ref[...] = jnp.zeros(acc_ref.shape, F32)

    for p in range(n_pages):
        copy(n, slot, p).wait()
    q = q_ref[0][:, :MLA_ROW]
    kt = _bf(buf[slot])
    s = _dot(q, kt) * MLA_SCALE
    m_old = m_ref[...]
    m_new = jnp.maximum(m_old, jnp.max(s, axis=1, keepdims=True))
    alpha = jnp.exp(m_old - m_new)
    e = jnp.exp(s - m_new)
    l_ref[...] = alpha * l_ref[...] + jnp.sum(e, axis=1, keepdims=True)
    acc_ref[...] = alpha * acc_ref[...] + _dot_nt(_bf(e), kt[:KV_LORA])
    m_ref[...] = m_new

    @pl.when(c == ncb - 1)
    def _():
        k_new = knew_ref[0]
        s_new = jnp.sum(q_ref[0].astype(F32) * k_new.astype(F32), axis=1, keepdims=True) * MLA_SCALE
        m_old = m_ref[...]
        m_new = jnp.maximum(m_old, s_new)
        alpha = jnp.exp(m_old - m_new)
        e_new = jnp.exp(s_new - m_new)
        l = alpha * l_ref[...] + e_new
        acc = alpha * acc_ref[...] + _bf(e_new).astype(F32) * k_new[:, :KV_LORA].astype(F32)
        o_ref[0] = _bf(acc / l)


def _s_mla(page_table, cache, q3, knew):
    db, n_pages = page_table.shape
    step_pages = min(MLA_S_PAGES, n_pages)
    ncb = n_pages // step_pages
    one = lambda *tail: pl.BlockSpec((1,) + tail, lambda b, c, pt: (b,) + (0,) * len(tail))
    return pl.pallas_call(
        _s_mla_body,
        grid_spec=pltpu.PrefetchScalarGridSpec(
            num_scalar_prefetch=1,
            grid=(db, ncb),
            in_specs=[pl.BlockSpec(memory_space=pl.ANY), one(MLA_HEADS, 256), one(1, 256)],
            out_specs=one(MLA_HEADS, KV_LORA),
            scratch_shapes=[pltpu.VMEM((2, MLA_ROW, step_pages * PAGE), F32),
                            pltpu.SemaphoreType.DMA((2,)), pltpu.VMEM((MLA_HEADS, 1), F32),
                            pltpu.VMEM((MLA_HEADS, 1), F32), pltpu.VMEM((MLA_HEADS, KV_LORA), F32)],
        ),
        out_shape=jax.ShapeDtypeStruct((db, MLA_HEADS, KV_LORA), BF16),
        compiler_params=_cparams(2),
        name="sample_mla",
    )(page_table, cache, q3, knew)


def _s_mem_body(q_ref, kv_ref, o_ref):
    q = q_ref[0]
    head = lax.broadcasted_iota(jnp.int32, (MEM_HEADS, 1), 0)
    out = jnp.zeros((MEM_HEADS, MEM_DH), F32)
    for h in range(MEM_HEADS):
        k = _bf(kv_ref[0, :, 0, h, :])
        v = _bf(kv_ref[0, :, 1, h, :])
        s = _dot_nt(q, k) * (MEM_DH ** -0.5)
        e = jnp.exp(s - jnp.max(s, axis=1, keepdims=True))
        p = e / jnp.sum(e, axis=1, keepdims=True)
        out = jnp.where(head == h, _dot(_bf(p), v), out)
    o_ref[0] = _bf(out)


def _s_mem(mq3, cache_mem):
    db = mq3.shape[0]
    one = lambda *tail: pl.BlockSpec((1,) + tail, lambda b: (b,) + (0,) * len(tail))
    return pl.pallas_call(
        _s_mem_body,
        grid=(db,),
        in_specs=[one(MEM_HEADS, MEM_DH), one(MEM_TOKENS, 2, MEM_HEADS, MEM_DH)],
        out_specs=one(MEM_HEADS, MEM_DH),
        out_shape=jax.ShapeDtypeStruct((db, MEM_HEADS, MEM_DH), BF16),
        compiler_params=_cparams(1),
        name="sample_mem",
    )(mq3, cache_mem)


def _sample_path(x_sample, cache_nsa_kv, cache_mla, state_nsa_win, cache_mem_kv, page_table, rel_bias,
                 ln_g, ln_b, ffn_w, pw, scw, mw):
    db = x_sample.shape[0]
    n_pages = page_table.shape[1]
    past = n_pages * PAGE
    wb = state_nsa_win.shape[2]
    n_blk = (past + 1 + SEL_BLOCK - 1) // SEL_BLOCK
    ln = lambda k: (ln_g[0, k][None], ln_b[0, k][None])
    x1 = _ffn(x_sample[:, 0], *ffn_w[0], *ln(0), db)
    qn, nkv4, win, gate, qmla, row, kmla, mq = _proj(
        x1, pw, _rope_tables(jnp.full((db,), past, jnp.int32)), db, False)
    q3 = qn.reshape(db, NSA_HEADS, NSA_DK)
    g3 = gate[:, :24].reshape(db, 3, NSA_HEADS).transpose(0, 2, 1)
    cache_t = jnp.transpose(cache_nsa_kv[0], (0, 2, 3, 4, 1)).reshape(-1, 512, PAGE)
    mla_t = jnp.transpose(cache_mla[0], (0, 2, 1))
    state_t = jnp.transpose(state_nsa_win[0], (0, 2, 3, 4, 1)).reshape(db, 256, wb)
    nc = past // CMP_BLOCK
    nbk = PAGE // CMP_BLOCK
    bc = _rel_bias(rel_bias, past - CMP_BLOCK * jnp.arange(nc) - (CMP_BLOCK - 1))
    bc = bc.reshape(NSA_HEADS, n_pages, nbk).transpose(0, 2, 1).reshape(NSA_HEADS, nc)
    bw = _rel_bias(rel_bias, wb - jnp.arange(wb))
    bn = _rel_bias(rel_bias, jnp.zeros((1,), jnp.int32))
    near = _rel_bias(rel_bias, PAGE - jnp.arange(PAGE))
    upper = jnp.arange(PAGE) >= SEL_BLOCK
    ta = jnp.where(upper, near, 0.0)
    tb = jnp.where(upper, 0.0, near)
    cmp = _s_compress(page_table, cache_t, scw)
    o8, imp = _s_cmp_win(cmp, q3, g3, state_t, win.reshape(db, 1, 256), bc, bw, bn)
    imp = imp.reshape(db, NSA_GROUPS, nbk, n_pages).transpose(0, 1, 3, 2).reshape(db, NSA_GROUPS, nc)
    idx = _s_topk(imp, n_blk)
    o8 = _s_select(page_table, idx.reshape(-1, db), cache_t, q3, g3, nkv4.reshape(db, 1, 512), o8, ta, tb, bn)
    o_nsa = o8.reshape(db, NSA_HEADS, NSA_GROUPS, NSA_DK)[:, jnp.arange(NSA_HEADS),
                                                           jnp.arange(NSA_HEADS) // NSA_HPG]
    o_nsa = o_nsa.reshape(db, 512).astype(BF16)
    o_lat = _s_mla(page_table, mla_t, qmla.reshape(db, MLA_HEADS, 256), kmla.reshape(db, 1, 256))
    o_mem = _s_mem(mq.reshape(db, MEM_HEADS, MEM_DH), cache_mem_kv[0])
    x2 = _merge(x1, o_nsa, o_lat.reshape(db, -1), o_mem.reshape(db, -1), mw, *ln(1), db)
    y = _ffn(x2, *ffn_w[1], *ln(2), db)
    new_win_t = jnp.concatenate([state_t[:, :, 1:], win[:, :, None]], axis=2)
    new_win = jnp.transpose(new_win_t.reshape(db, 2, NSA_GROUPS, NSA_DK, wb), (0, 4, 1, 2, 3))
    return (y[:, None],
            nkv4.reshape(1, db, 1, 4, NSA_GROUPS, NSA_DK),
            row.reshape(1, db, 1, MLA_ROW),
            new_win[None])


def kernel(x_prompt, x_sample, mem_prompt, cache_nsa_kv, cache_mla, state_nsa_win, cache_mem_kv, page_table, rel_bias, ln_g, ln_b, ffn_w1, ffn_w3, ffn_w2, w_in, nsa_cmp_pos, nsa_cmp_w1, nsa_cmp_w2, mla_g_q, mla_w_uq, mla_w_qr, mla_g_kv, mla_w_uk, mla_w_uv, mem_w_kv, w_br, w_o):
    assert ffn_w1.shape[0] == 1 and x_prompt.shape[0] == 1 and x_sample.shape[1] == 1
    ffn_w = [tuple(w[0, s].astype(BF16) for w in (ffn_w1, ffn_w3, ffn_w2)) for s in range(2)]
    pw = _proj_weights(w_in[0], mla_g_q[0], mla_w_uq[0], mla_w_qr[0], mla_g_kv[0], mla_w_uk[0])
    cw = _compress_weights(nsa_cmp_pos[0], nsa_cmp_w1[0], nsa_cmp_w2[0])
    mw = _merge_weights(w_in[0], mla_w_uv[0], w_br[0], w_o[0])
    yp, p_nsa, p_mla, p_win, p_mem = _prompt_path(x_prompt, mem_prompt, rel_bias, ln_g, ln_b, ffn_w, pw,
                                                  cw, mw, mem_w_kv)
    scw = _sample_compress_weights(nsa_cmp_pos[0], nsa_cmp_w1[0], nsa_cmp_w2[0])
    ys, s_nsa, s_mla, s_win = _sample_path(x_sample, cache_nsa_kv, cache_mla, state_nsa_win, cache_mem_kv,
                                           page_table, rel_bias, ln_g, ln_b, ffn_w, pw, scw, mw)
    return (yp, ys, p_nsa, p_mla, p_win, p_mem, s_nsa, s_mla, s_win)
```
